```python
import math, functools
import jax, jax.numpy as jnp
from jax import lax
import numpy as np

D_MODEL = 1024
BATCH = 32
SEQ = 2048
DEPTH = 2
DEC_BATCH = 16
DEC_SEQ = 32
PAST_LEN = 1024

CHUNK = 64
Q_BLOCK = 128
EPS = 1e-6
H_A = 8
A_NOPE = 64
A_ROPE = 32
A_V = 64
Q_LORA = 384
KV_LORA = 256
ROPE_THETA = 10000.0
H_B = 8
D_B = 64
H_C = 8
D_C = 64
H_IDX = 8
D_IDX = 64
TOPK_MAX = 256
NUM_BUCKETS = 32
MAX_DISTANCE = 256
N_BRANCH = 3
MIX_W = 512
D_FF = 2816
IN_SPLITS = (Q_LORA, KV_LORA, A_ROPE,
             H_B * D_B, H_B * D_B, H_B * D_B,
             H_C * D_C, D_C, D_C, H_IDX * D_IDX, D_IDX, H_IDX,
             D_MODEL, D_MODEL, D_MODEL)
IN_OFFSETS = tuple(int(o) for o in np.cumsum(IN_SPLITS)[:-1])
N_IN = int(sum(IN_SPLITS))

kernel_name = 'hybrid_mla_stickbreak_dsa_stream'


def rmsnorm(x, g):
    xf = x.astype(jnp.float32)
    y = xf * lax.rsqrt(jnp.mean(xf * xf, axis=-1, keepdims=True) + EPS)
    return (y * g.astype(jnp.float32)).astype(x.dtype)


def rope(x, pos):
    half = x.shape[-1] // 2
    inv = ROPE_THETA ** (-jnp.arange(half, dtype=jnp.float32) / half)
    ang = pos.astype(jnp.float32)[:, None] * inv[None, :]
    ang = ang.reshape((1, ang.shape[0]) + (1,) * (x.ndim - 3) + (half,))
    c, s = jnp.cos(ang), jnp.sin(ang)
    xf = x.astype(jnp.float32)
    x1, x2 = xf[..., :half], xf[..., half:]
    return jnp.concatenate([x1 * c - x2 * s, x1 * s + x2 * c], axis=-1).astype(x.dtype)


def chunk_visible(q_pos, k_pos):
    return (k_pos // CHUNK) <= (q_pos // CHUNK)


def t5_bucket(rel):
    nb = NUM_BUCKETS // 2
    max_exact = nb // 2
    ret = jnp.where(rel > 0, nb, 0)
    n = jnp.abs(rel)
    nf = jnp.maximum(n, 1).astype(jnp.float32)
    large = max_exact + (jnp.log(nf / max_exact) / math.log(MAX_DISTANCE / max_exact)
                         * (nb - max_exact)).astype(jnp.int32)
    large = jnp.minimum(large, nb - 1)
    return ret + jnp.where(n < max_exact, n, large)


def sweep(attend, q_args, kv_args, q_pos, k_pos):
    t_q = q_pos.shape[0]
    if t_q <= Q_BLOCK:
        return attend(q_args, kv_args, q_pos, k_pos)
    past = k_pos.shape[0] - t_q
    outs = []
    for s in range(0, t_q, Q_BLOCK):
        e = min(s + Q_BLOCK, t_q)
        outs.append(attend(tuple(a[:, s:e] for a in q_args),
                           tuple(a[:, :past + e] for a in kv_args),
                           q_pos[s:e], k_pos[:past + e]))
    return jnp.concatenate(outs, axis=1)


def mla_attend(q_args, kv_args, q_pos, k_pos):
    q_nope, q_rope = q_args
    k_nope, k_rope, v = kv_args
    s = (jnp.einsum('bqhd,bkhd->bhqk', q_nope, k_nope)
         + jnp.einsum('bqhd,bkd->bhqk', q_rope, k_rope)).astype(jnp.float32)
    s = s * (A_NOPE + A_ROPE) ** -0.5
    s = jnp.where(chunk_visible(q_pos[:, None], k_pos[None, :]), s, -jnp.inf)
    p = jax.nn.softmax(s, axis=-1)
    return jnp.einsum('bhqk,bkhd->bqhd', p.astype(v.dtype), v)


def sb_attend(q_args, kv_args, q_pos, k_pos):
    (q,) = q_args
    k, v = kv_args
    z = jnp.einsum('bqhd,bkhd->bhqk', q, k).astype(jnp.float32) * D_B ** -0.5
    strict = k_pos[None, :] < q_pos[:, None]
    log_beta = jax.nn.log_sigmoid(z)
    log_keep = jnp.where(strict, jax.nn.log_sigmoid(-z), 0.0)
    log_stick = lax.cumsum(log_keep, axis=3, reverse=True) - log_keep
    a = jnp.where(strict, jnp.exp(log_beta + log_stick), 0.0)
    return jnp.einsum('bhqk,bkhd->bqhd', a, v.astype(jnp.float32)).astype(v.dtype)


def dsa_attend(topk, rel_bias, q_args, kv_args, q_pos, k_pos):
    q, qi, wi = q_args
    k, v, ki = kv_args
    vis = chunk_visible(q_pos[:, None], k_pos[None, :])
    rel = jax.nn.relu(jnp.einsum('bqhd,bkd->bqhk', qi, ki).astype(jnp.float32) * D_IDX ** -0.5)
    score = jnp.einsum('bqhk,bqh->bqk', rel, wi.astype(jnp.float32) * H_IDX ** -0.5)
    score = jnp.where(vis[None], score, -jnp.inf)
    kk = min(topk, k_pos.shape[0])
    _, sel = lax.top_k(score, kk)
    gather = jax.vmap(lambda a, i: a[i])
    k_sel = gather(k, sel)
    v_sel = gather(v, sel)
    pos_sel = k_pos[sel]
    valid = chunk_visible(q_pos[None, :, None], pos_sel)
    bias = rel_bias[t5_bucket(pos_sel - q_pos[None, :, None])]
    logits = (jnp.einsum('bqhd,bqjd->bhqj', q, k_sel).astype(jnp.float32) * D_C ** -0.5
              + jnp.moveaxis(bias.astype(jnp.float32), -1, 1))
    logits = jnp.where(valid[:, None], logits, -jnp.inf)
    p = jax.nn.softmax(logits, axis=-1)
    return jnp.einsum('bhqj,bqjd->bqhd', p.astype(v.dtype), v_sel)


def ffn_half(x, pre_g, w_up, w_down, post_g):
    h = rmsnorm(x, pre_g)
    gate, up = jnp.split(h @ w_up, 2, axis=-1)
    y = (jax.nn.silu(gate) * up) @ w_down
    return x + 0.5 * rmsnorm(y, post_g)


def mixer(x, l, past, P, rel_bias):
    b, t, _ = x.shape
    p_len = 0 if past is None else past[0].shape[1]
    q_pos = p_len + jnp.arange(t, dtype=jnp.int32)
    k_pos = jnp.arange(p_len + t, dtype=jnp.int32)
    u = rmsnorm(x, P['mix_pre_gain'][l])
    (q_lat, kv_lat, k_r, q_b, k_b, v_b, q_c, k_c, v_c, q_i, k_i, w_i,
     g_a, g_b, g_c) = jnp.split(u @ P['w_in'][l], IN_OFFSETS, axis=-1)
    c_q = rmsnorm(q_lat, P['mla_q_norm'][l])
    qa = (c_q @ P['w_mla_uq'][l]).reshape(b, t, H_A, A_NOPE + A_ROPE)
    qa_nope, qa_rope = qa[..., :A_NOPE], rope(qa[..., A_NOPE:], q_pos)
    c_kv = rmsnorm(kv_lat, P['mla_kv_norm'][l])
    k_rope = rope(k_r, q_pos)
    new_rows = (c_kv, k_rope,
                k_b.reshape(b, t, H_B, D_B), v_b.reshape(b, t, H_B, D_B),
                k_c, v_c, k_i)
    if past is None:
        full = new_rows
    else:
        full = tuple(jnp.concatenate([pa, nr], axis=1) for pa, nr in zip(past, new_rows))
    ckv_all, krope_all, kb_all, vb_all, kc_all, vc_all, ki_all = full
    kv = (ckv_all @ P['w_mla_ukv'][l]).reshape(b, -1, H_A, A_NOPE + A_V)
    o_a = sweep(mla_attend, (qa_nope, qa_rope),
                (kv[..., :A_NOPE], krope_all, kv[..., A_NOPE:]), q_pos, k_pos)
    o_b = sweep(sb_attend, (q_b.reshape(b, t, H_B, D_B),), (kb_all, vb_all), q_pos, k_pos)
    topk = min(TOPK_MAX, k_pos.shape[0] // 4)
    o_c = sweep(functools.partial(dsa_attend, topk, rel_bias),
                (q_c.reshape(b, t, H_C, D_C), q_i.reshape(b, t, H_IDX, D_IDX), w_i),
                (kc_all, vc_all, ki_all), q_pos, k_pos)
    w_br = P['w_branch'][l]
    merged = (jax.nn.sigmoid(g_a) * (o_a.reshape(b, t, MIX_W) @ w_br[0])
              + jax.nn.sigmoid(g_b) * (o_b.reshape(b, t, MIX_W) @ w_br[1])
              + jax.nn.sigmoid(g_c) * (o_c.reshape(b, t, MIX_W) @ w_br[2]))
    y = merged @ P['w_o'][l]
    return x + rmsnorm(y, P['mix_post_gain'][l]), new_rows


def trunk(x, past, P, rel_bias):
    rows = []
    for l in range(DEPTH):
        x = ffn_half(x, P['ffn_pre_gain'][l, 0], P['w_ffn_up'][l, 0],
                     P['w_ffn_down'][l, 0], P['ffn_post_gain'][l, 0])
        x, new = mixer(x, l, None if past is None else past[l], P, rel_bias)
        x = ffn_half(x, P['ffn_pre_gain'][l, 1], P['w_ffn_up'][l, 1],
                     P['w_ffn_down'][l, 1], P['ffn_post_gain'][l, 1])
        rows.append(new)
    return x, tuple(jnp.stack(group) for group in zip(*rows))


def setup_inputs(seed: int = 0) -> dict:
    key = jax.random.key(seed)
    ks = jax.random.split(key, 24)
    def nrm(k, shape, scale):
        return jax.random.normal(k, shape, jnp.float32) * scale
    def gain(k, shape):
        return 1.0 + 0.02 * jax.random.normal(k, shape, jnp.float32)
    return {
        'x_prompt': nrm(ks[0], (BATCH, SEQ, D_MODEL), 1.0),
        'x_sample': nrm(ks[1], (DEC_BATCH, DEC_SEQ, D_MODEL), 1.0),
        'cache_mla_ckv': nrm(ks[2], (DEPTH, DEC_BATCH, PAST_LEN, KV_LORA), 1.0),
        'cache_mla_krope': nrm(ks[3], (DEPTH, DEC_BATCH, PAST_LEN, A_ROPE), 1.0),
        'cache_sb_k': nrm(ks[4], (DEPTH, DEC_BATCH, PAST_LEN, H_B, D_B), 1.0),
        'cache_sb_v': nrm(ks[5], (DEPTH, DEC_BATCH, PAST_LEN, H_B, D_B), 1.0),
        'cache_dsa_k': nrm(ks[6], (DEPTH, DEC_BATCH, PAST_LEN, D_C), 1.0),
        'cache_dsa_v': nrm(ks[7], (DEPTH, DEC_BATCH, PAST_LEN, D_C), 1.0),
        'cache_dsa_kidx': nrm(ks[8], (DEPTH, DEC_BATCH, PAST_LEN, D_IDX), 1.0),
        'rel_bias': nrm(ks[9], (NUM_BUCKETS, H_C), 0.3),
        'ffn_pre_gain': gain(ks[10], (DEPTH, 2, D_MODEL)),
        'w_ffn_up': nrm(ks[11], (DEPTH, 2, D_MODEL, 2 * D_FF), D_MODEL ** -0.5),
        'w_ffn_down': nrm(ks[12], (DEPTH, 2, D_FF, D_MODEL), D_FF ** -0.5),
        'ffn_post_gain': gain(ks[13], (DEPTH, 2, D_MODEL)),
        'mix_pre_gain': gain(ks[14], (DEPTH, D_MODEL)),
        'w_in': nrm(ks[15], (DEPTH, D_MODEL, N_IN), D_MODEL ** -0.5),
        'mla_q_norm': gain(ks[16], (DEPTH, Q_LORA)),
        'w_mla_uq': nrm(ks[17], (DEPTH, Q_LORA, H_A * (A_NOPE + A_ROPE)), Q_LORA ** -0.5),
        'mla_kv_norm': gain(ks[18], (DEPTH, KV_LORA)),
        'w_mla_ukv': nrm(ks[19], (DEPTH, KV_LORA, H_A * (A_NOPE + A_V)), KV_LORA ** -0.5),
        'w_branch': nrm(ks[20], (DEPTH, N_BRANCH, MIX_W, D_MODEL), MIX_W ** -0.5),
        'w_o': nrm(ks[21], (DEPTH, D_MODEL, D_MODEL), D_MODEL ** -0.5),
        'mix_post_gain': gain(ks[22], (DEPTH, D_MODEL)),
    }


def reference(x_prompt, x_sample, cache_mla_ckv, cache_mla_krope, cache_sb_k, cache_sb_v,
              cache_dsa_k, cache_dsa_v, cache_dsa_kidx, rel_bias,
              ffn_pre_gain, w_ffn_up, w_ffn_down, ffn_post_gain,
              mix_pre_gain, w_in, mla_q_norm, w_mla_uq, mla_kv_norm, w_mla_ukv,
              w_branch, w_o, mix_post_gain):
    P = dict(ffn_pre_gain=ffn_pre_gain, w_ffn_up=w_ffn_up, w_ffn_down=w_ffn_down,
             ffn_post_gain=ffn_post_gain, mix_pre_gain=mix_pre_gain, w_in=w_in,
             mla_q_norm=mla_q_norm, w_mla_uq=w_mla_uq, mla_kv_norm=mla_kv_norm,
             w_mla_ukv=w_mla_ukv, w_branch=w_branch, w_o=w_o, mix_post_gain=mix_post_gain)
    y_prompt, (p_ckv, p_krope, p_sbk, p_sbv, p_dk, p_dv, p_dki) = trunk(x_prompt, None, P, rel_bias)
    caches = (cache_mla_ckv, cache_mla_krope, cache_sb_k, cache_sb_v,
              cache_dsa_k, cache_dsa_v, cache_dsa_kidx)
    past = [tuple(c[l] for c in caches) for l in range(DEPTH)]
    y_sample, (s_ckv, s_krope, s_sbk, s_sbv, s_dk, s_dv, s_dki) = trunk(x_sample, past, P, rel_bias)
    return (y_prompt, y_sample,
            p_ckv, p_krope, p_sbk, p_sbv, p_dk, p_dv, p_dki,
            s_ckv, s_krope, s_sbk, s_sbv, s_dk, s_dv, s_dki)
```

```python
import functools
import math

import numpy as np
import jax
import jax.numpy as jnp
from jax import lax
from jax.experimental import pallas as pl
from jax.experimental.pallas import tpu as pltpu

D_MODEL = 1024
CHUNK = 64
CHUNK_SHIFT = 6
EPS = 1e-6
H_A, A_NOPE, A_ROPE, A_V = 8, 64, 32, 64
Q_LORA, KV_LORA = 384, 256
ROPE_THETA = 10000.0
H_B, D_B = 8, 64
H_C, D_C = 8, 64
H_IDX, D_IDX = 8, 64
TOPK_MAX = 256
NUM_BUCKETS, MAX_DISTANCE = 32, 256
MIX_W = 512
D_FF = 2816
IN_SPLITS = (Q_LORA, KV_LORA, A_ROPE, H_B * D_B, H_B * D_B, H_B * D_B,
             H_C * D_C, D_C, D_C, H_IDX * D_IDX, D_IDX, H_IDX,
             D_MODEL, D_MODEL, D_MODEL)
IN_OFFSETS = tuple(int(o) for o in np.cumsum((0,) + IN_SPLITS))

LANE = 128
KC = 128
FF_CHUNK = 256
NEG = -1e30
INT_MIN = -2 ** 31
VMEM_LIMIT = 56 * 1024 * 1024
BF16 = jnp.bfloat16
F32 = jnp.float32


def _cparams(*sem):
    return pltpu.CompilerParams(dimension_semantics=sem, vmem_limit_bytes=VMEM_LIMIT)


def _const_spec(shape):
    nd = len(shape)
    return pl.BlockSpec(shape, lambda *_: (0,) * nd, pipeline_mode=pl.Buffered(1))


def _rms(x, g):
    return x * lax.rsqrt(jnp.mean(x * x, axis=-1, keepdims=True) + EPS) * g


def _dot(a, b):
    return jnp.dot(a, b, preferred_element_type=F32)


def _dot_nt(a, b):
    return lax.dot_general(a, b, (((1,), (1,)), ((), ())), preferred_element_type=F32)


def _ffn_kernel(x_ref, pre_ref, wup_ref, wdn_ref, post_ref, o_ref):
    x = x_ref[...]
    h = _rms(x, pre_ref[...]).astype(BF16)
    acc = jnp.zeros(x.shape, F32)
    for c in range(D_FF // FF_CHUNK):
        lo = c * FF_CHUNK
        gate = _dot(h, wup_ref[:, lo:lo + FF_CHUNK])
        up = _dot(h, wup_ref[:, D_FF + lo:D_FF + lo + FF_CHUNK])
        act = (gate * jax.nn.sigmoid(gate) * up).astype(BF16)
        acc = acc + _dot(act, wdn_ref[lo:lo + FF_CHUNK, :])
    o_ref[...] = x + 0.5 * _rms(acc, post_ref[...])


def _ffn_half(x, pre_g, w_up, w_down, post_g, tm):
    n = x.shape[0]
    row = pl.BlockSpec((tm, D_MODEL), lambda i: (i, 0))
    return pl.pallas_call(
        _ffn_kernel,
        grid=(n // tm,),
        in_specs=[row, _const_spec((1, D_MODEL)), _const_spec((D_MODEL, 2 * D_FF)),
                  _const_spec((D_FF, D_MODEL)), _const_spec((1, D_MODEL))],
        out_specs=row,
        out_shape=jax.ShapeDtypeStruct((n, D_MODEL), F32),
        compiler_params=_cparams("parallel"),
        name="ffn_half",
    )(x, pre_g, w_up, w_down, post_g)


_P_QLAT = 0
_P_KVLAT = _P_QLAT + Q_LORA
_P_KR = _P_KVLAT + KV_LORA
_P_KRSW = _P_KR + LANE
_P_WI = _P_KRSW + LANE
_P_QB = _P_WI + LANE
_P_KB = _P_QB + MIX_W
_P_VB = _P_KB + MIX_W
_P_QC = _P_VB + MIX_W
_P_QI = _P_QC + MIX_W
_P_KC = _P_QI + MIX_W
_P_VC = _P_KC + LANE
_P_KI = _P_VC + LANE
_P_END = _P_KI + LANE


def _in_kernel(x_ref, g_ref, w_ref, qn_ref, wuq_ref, kvn_ref, cq_ref, sq_ref, ck_ref, sk_ref,
               ckv_o, krope_o, kb_o, vb_o, kc_o, vc_o, ki_o, wi_o,
               qan_o, qar_o, qb_o, qc_o, qi_o):
    u = _rms(x_ref[...], g_ref[...]).astype(BF16)
    p = _dot(u, w_ref[...])
    c_q = _rms(p[:, _P_QLAT:_P_QLAT + Q_LORA], qn_ref[...]).astype(BF16)
    qa = _dot(c_q, wuq_ref[...])
    n_nope, n_rope = H_A * A_NOPE, H_A * A_ROPE
    scale_a = (A_NOPE + A_ROPE) ** -0.5
    qan_o[...] = (qa[:, :n_nope] * scale_a).astype(BF16)
    q_rot = qa[:, n_nope:n_nope + n_rope] * cq_ref[...] + qa[:, n_nope + n_rope:] * sq_ref[...]
    qar_o[...] = (q_rot * scale_a).astype(BF16)
    ckv_o[...] = _rms(p[:, _P_KVLAT:_P_KVLAT + KV_LORA], kvn_ref[...])
    krope_o[...] = (p[:, _P_KR:_P_KR + A_ROPE] * ck_ref[...]
                    + p[:, _P_KRSW:_P_KRSW + A_ROPE] * sk_ref[...])
    qb_o[...] = (p[:, _P_QB:_P_QB + MIX_W] * D_B ** -0.5).astype(BF16)
    kb_o[...] = p[:, _P_KB:_P_KB + MIX_W]
    vb_o[...] = p[:, _P_VB:_P_VB + MIX_W]
    qc_o[...] = (p[:, _P_QC:_P_QC + MIX_W] * D_C ** -0.5).astype(BF16)
    qi_o[...] = (p[:, _P_QI:_P_QI + MIX_W] * D_IDX ** -0.5).astype(BF16)
    kc_o[...] = p[:, _P_KC:_P_KC + D_C]
    vc_o[...] = p[:, _P_VC:_P_VC + D_C]
    ki_o[...] = p[:, _P_KI:_P_KI + D_IDX]
    wi_o[...] = p[:, _P_WI:_P_WI + H_IDX] * H_IDX ** -0.5


def _mixer_in(x, g, w_pack, qn, wuq_pack, kvn, cq, sq, ck, sk, tm):
    n = x.shape[0]
    n_tab = cq.shape[0] // tm

    def row(w):
        return pl.BlockSpec((tm, w), lambda i: (i, 0))

    def tab(w):
        return pl.BlockSpec((tm, w), lambda i: (i % n_tab, 0))

    widths_f32 = (KV_LORA, A_ROPE, MIX_W, MIX_W, D_C, D_C, D_IDX, H_IDX)
    widths_bf16 = (H_A * A_NOPE, H_A * A_ROPE, MIX_W, MIX_W, MIX_W)
    return pl.pallas_call(
        _in_kernel,
        grid=(n // tm,),
        in_specs=[row(D_MODEL), _const_spec((1, D_MODEL)), _const_spec(w_pack.shape),
                  _const_spec((1, Q_LORA)), _const_spec(wuq_pack.shape), _const_spec((1, KV_LORA)),
                  tab(H_A * A_ROPE), tab(H_A * A_ROPE), tab(A_ROPE), tab(A_ROPE)],
        out_specs=[row(w) for w in widths_f32 + widths_bf16],
        out_shape=([jax.ShapeDtypeStruct((n, w), F32) for w in widths_f32]
                   + [jax.ShapeDtypeStruct((n, w), BF16) for w in widths_bf16]),
        compiler_params=_cparams("parallel"),
        name="mixer_in",
    )(x, g, w_pack, qn, wuq_pack, kvn, cq, sq, ck, sk)


def _ukv_kernel(c_ref, w_ref, k_o, v_o):
    kv = _dot(c_ref[...].astype(BF16), w_ref[...])
    k_o[...] = kv[:, :H_A * A_NOPE].astype(BF16)
    v_o[...] = kv[:, H_A * A_NOPE:].astype(BF16)


def _ukv(ckv, w_pack, tm):
    n = ckv.shape[0]
    return pl.pallas_call(
        _ukv_kernel,
        grid=(n // tm,),
        in_specs=[pl.BlockSpec((tm, KV_LORA), lambda i: (i, 0)), _const_spec(w_pack.shape)],
        out_specs=[pl.BlockSpec((tm, H_A * A_NOPE), lambda i: (i, 0)),
                   pl.BlockSpec((tm, H_A * A_V), lambda i: (i, 0))],
        out_shape=[jax.ShapeDtypeStruct((n, H_A * A_NOPE), BF16),
                   jax.ShapeDtypeStruct((n, H_A * A_V), BF16)],
        compiler_params=_cparams("parallel"),
        name="mla_ukv",
    )(ckv, w_pack)


def _num_key_steps(q0, tq, kv_len):
    k_end = jnp.minimum(kv_len, ((q0 + tq - 1) // CHUNK + 1) * CHUNK)
    return (k_end + KC - 1) // KC


def _positions(q0, k0, tq):
    qpos = q0 + lax.broadcasted_iota(jnp.int32, (tq, KC), 0)
    kpos = k0 + lax.broadcasted_iota(jnp.int32, (tq, KC), 1)
    return qpos, kpos


def _mla_kernel(q_ref, k_ref, v_ref, o_ref, m_ref, l_ref, acc_ref, *, tq, q_off, kv_len):
    q0 = q_off + pl.program_id(1) * tq
    nk = _num_key_steps(q0, tq, kv_len)
    m_ref[...] = jnp.full(m_ref.shape, NEG, F32)
    l_ref[...] = jnp.zeros(l_ref.shape, F32)
    acc_ref[...] = jnp.zeros(acc_ref.shape, F32)

    def body(c, carry):
        k0 = pl.multiple_of(c * KC, KC)
        qpos, kpos = _positions(q0, k0, tq)
        vis = ((kpos >> CHUNK_SHIFT) <= (qpos >> CHUNK_SHIFT)) & (kpos < kv_len)
        for h in range(H_A):
            s = _dot_nt(q_ref[0, h], k_ref[0, h, pl.ds(k0, KC), :])
            s = jnp.where(vis, s, NEG)
            m_prev = m_ref[h]
            m_new = jnp.maximum(m_prev, jnp.max(s, axis=1, keepdims=True))
            alpha = jnp.exp(m_prev - m_new)
            p = jnp.exp(s - m_new)
            l_ref[h] = alpha * l_ref[h] + jnp.sum(p, axis=1, keepdims=True)
            acc_ref[h] = alpha * acc_ref[h] + _dot(p.astype(BF16), v_ref[0, h, pl.ds(k0, KC), :])
            m_ref[h] = m_new
        return carry

    lax.fori_loop(0, nk, body, 0)
    o_ref[0] = (acc_ref[...] / l_ref[...]).astype(o_ref.dtype)


def _mla_attention(q, k, v, tq, q_off, kv_len):
    b, h, t, dq = q.shape
    tk, dv = k.shape[2], v.shape[3]
    return pl.pallas_call(
        functools.partial(_mla_kernel, tq=tq, q_off=q_off, kv_len=kv_len),
        grid=(b, t // tq),
        in_specs=[pl.BlockSpec((1, h, tq, dq), lambda i, j: (i, 0, j, 0)),
                  pl.BlockSpec((1, h, tk, dq), lambda i, j: (i, 0, 0, 0)),
                  pl.BlockSpec((1, h, tk, dv), lambda i, j: (i, 0, 0, 0))],
        out_specs=pl.BlockSpec((1, h, tq, dv), lambda i, j: (i, 0, j, 0)),
        out_shape=jax.ShapeDtypeStruct((b, h, t, dv), BF16),
        scratch_shapes=[pltpu.VMEM((h, tq, 1), F32), pltpu.VMEM((h, tq, 1), F32),
                        pltpu.VMEM((h, tq, dv), F32)],
        compiler_params=_cparams("parallel", "arbitrary"),
        name="mla_attention",
    )(q, k, v)


def _sb_kernel(q_ref, k_ref, v_ref, o_ref, run_ref, acc_ref, *, tq, q_off, kv_len):
    q0 = q_off + pl.program_id(1) * tq
    nk = _num_key_steps(q0, tq, kv_len)
    run_ref[...] = jnp.zeros(run_ref.shape, F32)
    acc_ref[...] = jnp.zeros(acc_ref.shape, F32)
    later = jnp.where(lax.broadcasted_iota(jnp.int32, (KC, KC), 0)
                      > lax.broadcasted_iota(jnp.int32, (KC, KC), 1), 1.0, 0.0).astype(BF16)

    def body(i, carry):
        k0 = pl.multiple_of((nk - 1 - i) * KC, KC)
        qpos, kpos = _positions(q0, k0, tq)
        strict = kpos < qpos
        for h in range(H_B):
            z = _dot_nt(q_ref[0, h], k_ref[0, h, pl.ds(k0, KC), :])
            soft = jnp.log(1.0 + jnp.exp(-jnp.abs(z)))
            log_beta = jnp.minimum(z, 0.0) - soft
            log_keep = jnp.where(strict, jnp.minimum(-z, 0.0) - soft, 0.0)
            hi = log_keep.astype(BF16)
            lo = (log_keep - hi.astype(F32)).astype(BF16)
            stick = _dot(hi, later) + _dot(lo, later) + run_ref[h]
            a = jnp.where(strict, jnp.exp(log_beta + stick), 0.0)
            acc_ref[h] = acc_ref[h] + _dot(a.astype(BF16), v_ref[0, h, pl.ds(k0, KC), :])
            run_ref[h] = run_ref[h] + jnp.sum(log_keep, axis=1, keepdims=True)
        return carry

    lax.fori_loop(0, nk, body, 0)
    o_ref[0] = acc_ref[...].astype(o_ref.dtype)


def _sb_attention(q, k, v, tq, q_off, kv_len):
    b, h, t, d = q.shape
    tk = k.shape[2]
    return pl.pallas_call(
        functools.partial(_sb_kernel, tq=tq, q_off=q_off, kv_len=kv_len),
        grid=(b, t // tq),
        in_specs=[pl.BlockSpec((1, h, tq, d), lambda i, j: (i, 0, j, 0)),
                  pl.BlockSpec((1, h, tk, d), lambda i, j: (i, 0, 0, 0)),
                  pl.BlockSpec((1, h, tk, d), lambda i, j: (i, 0, 0, 0))],
        out_specs=pl.BlockSpec((1, h, tq, d), lambda i, j: (i, 0, j, 0)),
        out_shape=jax.ShapeDtypeStruct((b, h, t, d), BF16),
        scratch_shapes=[pltpu.VMEM((h, tq, 1), F32), pltpu.VMEM((h, tq, d), F32)],
        compiler_params=_cparams("parallel", "arbitrary"),
        name="sb_attention",
    )(q, k, v)


def _t5_bucket_np(rel):
    nb = NUM_BUCKETS // 2
    max_exact = nb // 2
    ret = np.where(rel > 0, nb, 0)
    n = np.abs(rel)
    nf = np.maximum(n, 1).astype(np.float32)
    large = max_exact + (np.log(nf / max_exact) / math.log(MAX_DISTANCE / max_exact)
                         * (nb - max_exact)).astype(np.int32)
    large = np.minimum(large, nb - 1)
    return (ret + np.where(n < max_exact, n, large)).astype(np.int32)


def _bias_kernel(rb_ref, bkt_ref, o_ref, *, tq):
    bkt = bkt_ref[0]
    for h in range(H_C):
        acc = jnp.zeros(bkt.shape, F32)
        for b in range(NUM_BUCKETS):
            acc = jnp.where(bkt == b, rb_ref[b, h], acc)
        o_ref[0, h * tq:(h + 1) * tq, :] = acc


def _bias_tiles(rel_bias, tq, n_off):
    i = np.arange(tq)[None, :, None]
    j = np.arange(KC)[None, None, :]
    d = np.arange(n_off)[:, None, None]
    buckets = jnp.asarray(_t5_bucket_np(j - i - d * KC))
    return pl.pallas_call(
        functools.partial(_bias_kernel, tq=tq),
        grid=(n_off,),
        in_specs=[pl.BlockSpec(memory_space=pltpu.SMEM),
                  pl.BlockSpec((1, tq, KC), lambda d: (d, 0, 0))],
        out_specs=pl.BlockSpec((1, H_C * tq, KC), lambda d: (d, 0, 0)),
        out_shape=jax.ShapeDtypeStruct((n_off, H_C * tq, KC), F32),
        compiler_params=_cparams("parallel"),
        name="dsa_bias_tiles",
    )(rel_bias, buckets)


def _dsa_kernel(q_ref, qi_ref, wi_ref, k_ref, v_ref, ki_ref, bias_ref, o_ref,
                key_ref, m_ref, l_ref, acc_ref, *, tq, q_off, kv_len):
    q0 = q_off + pl.program_id(1) * tq
    nk = _num_key_steps(q0, tq, kv_len)
    topk = min(TOPK_MAX, kv_len // 4)
    hq = H_C * tq

    def score_body(c, carry):
        k0 = pl.multiple_of(c * KC, KC)
        qpos, kpos = _positions(q0, k0, tq)
        vis = ((kpos >> CHUNK_SHIFT) <= (qpos >> CHUNK_SHIFT)) & (kpos < kv_len)
        r = _dot_nt(qi_ref[0, 0], ki_ref[0, pl.ds(k0, KC), :])
        r = jnp.maximum(r, 0.0) * wi_ref[0, 0]
        score = jnp.sum(r.reshape(H_IDX, tq, KC), axis=0) + 0.0
        bits = lax.bitcast_convert_type(score, jnp.int32)
        key = bits ^ ((bits >> 31) & 0x7FFFFFFF)
        key_ref[:, pl.ds(k0, KC)] = jnp.where(vis, key, INT_MIN)
        return carry

    lax.fori_loop(0, nk, score_body, 0)

    def count(pred):
        def body(c, acc):
            k0 = pl.multiple_of(c * KC, KC)
            return acc + jnp.where(pred(key_ref[:, pl.ds(k0, KC)]), 1, 0)
        acc = lax.fori_loop(0, nk, body, jnp.zeros((tq, KC), jnp.int32))
        return jnp.sum(acc, axis=1, keepdims=True)

    def search(i, thr):
        cand = thr + jnp.left_shift(jnp.int32(1), 31 - i)
        cnt = count(lambda key: key >= cand)
        return jnp.where(cnt >= topk, cand, thr)

    thr = lax.fori_loop(0, 32, search, jnp.full((tq, 1), INT_MIN, jnp.int32))

    n_ge = count(lambda key: key >= thr)
    surplus = jnp.where(thr > INT_MIN, n_ge - topk, 0)

    @pl.when(jnp.max(surplus) > 0)
    def _():
        need = topk - count(lambda key: key > thr)
        upto = jnp.where(lax.broadcasted_iota(jnp.int32, (KC, KC), 0)
                         <= lax.broadcasted_iota(jnp.int32, (KC, KC), 1), 1.0, 0.0).astype(BF16)

        def body(c, seen):
            k0 = pl.multiple_of(c * KC, KC)
            key = key_ref[:, pl.ds(k0, KC)]
            tie = key == thr
            rank = seen + _dot(jnp.where(tie, 1.0, 0.0).astype(BF16), upto).astype(jnp.int32)
            drop = tie & (rank > need)
            key_ref[:, pl.ds(k0, KC)] = jnp.where(drop, INT_MIN, key)
            return seen + jnp.sum(jnp.where(tie, 1, 0), axis=1, keepdims=True)

        lax.fori_loop(0, nk, body, jnp.zeros((tq, 1), jnp.int32))

    thr_sel = jnp.maximum(thr, INT_MIN + 1)
    m_ref[...] = jnp.full(m_ref.shape, NEG, F32)
    l_ref[...] = jnp.zeros(l_ref.shape, F32)
    acc_ref[...] = jnp.zeros(acc_ref.shape, F32)
    d0 = q0 // KC

    def attend(c, carry):
        k0 = pl.multiple_of(c * KC, KC)
        s = _dot_nt(q_ref[0, 0], k_ref[0, pl.ds(k0, KC), :]) + bias_ref[d0 - c]
        sel = key_ref[:, pl.ds(k0, KC)] >= thr_sel
        s = jnp.where(sel[None], s.reshape(H_C, tq, KC), NEG).reshape(hq, KC)
        m_prev = m_ref[...]
        m_new = jnp.maximum(m_prev, jnp.max(s, axis=1, keepdims=True))
        alpha = jnp.exp(m_prev - m_new)
        p = jnp.exp(s - m_new)
        l_ref[...] = alpha * l_ref[...] + jnp.sum(p, axis=1, keepdims=True)
        acc_ref[...] = alpha * acc_ref[...] + _dot(p.astype(BF16), v_ref[0, pl.ds(k0, KC), :])
        m_ref[...] = m_new
        return carry

    lax.fori_loop(0, nk, attend, 0)
    o_ref[0, 0] = (acc_ref[...] / l_ref[...]).astype(o_ref.dtype)


def _dsa_attention(q, qi, wi, k, v, ki, bias, tq, q_off, kv_len):
    b, nq, hq, d = q.shape
    tk = k.shape[1]
    qspec = pl.BlockSpec((1, 1, hq, d), lambda i, j: (i, j, 0, 0))
    kspec = pl.BlockSpec((1, tk, d), lambda i, j: (i, 0, 0))
    return pl.pallas_call(
        functools.partial(_dsa_kernel, tq=tq, q_off=q_off, kv_len=kv_len),
        grid=(b, nq),
        in_specs=[qspec, qspec, pl.BlockSpec((1, 1, hq, 1), lambda i, j: (i, j, 0, 0)),
                  kspec, kspec, kspec, _const_spec(bias.shape)],
        out_specs=qspec,
        out_shape=jax.ShapeDtypeStruct((b, nq, hq, d), BF16),
        scratch_shapes=[pltpu.VMEM((tq, tk), jnp.int32), pltpu.VMEM((hq, 1), F32),
                        pltpu.VMEM((hq, 1), F32), pltpu.VMEM((hq, d), F32)],
        compiler_params=_cparams("parallel", "arbitrary"),
        name="dsa_attention",
    )(q, qi, wi, k, v, ki, bias)


def _merge_kernel(x_ref, oa_ref, ob_ref, oc_ref, g_ref, wg_ref, wbr_ref, wo_ref, post_ref, o_ref):
    x = x_ref[...]
    u = _rms(x, g_ref[...]).astype(BF16)
    merged = jnp.zeros(x.shape, F32)
    for i, o_br in enumerate((oa_ref, ob_ref, oc_ref)):
        gate = jax.nn.sigmoid(_dot(u, wg_ref[:, i * D_MODEL:(i + 1) * D_MODEL]))
        merged = merged + gate * _dot(o_br[...], wbr_ref[i])
    y = _dot(merged.astype(BF16), wo_ref[...])
    o_ref[...] = x + _rms(y, post_ref[...])


def _merge(x, oa, ob, oc, g, w_gate, w_br, w_o, post_g, tm):
    n = x.shape[0]
    row = pl.BlockSpec((tm, D_MODEL), lambda i: (i, 0))
    br = pl.BlockSpec((tm, MIX_W), lambda i: (i, 0))
    return pl.pallas_call(
        _merge_kernel,
        grid=(n // tm,),
        in_specs=[row, br, br, br, _const_spec((1, D_MODEL)), _const_spec(w_gate.shape),
                  _const_spec(w_br.shape), _const_spec(w_o.shape), _const_spec((1, D_MODEL))],
        out_specs=row,
        out_shape=jax.ShapeDtypeStruct((n, D_MODEL), F32),
        compiler_params=_cparams("parallel"),
        name="mixer_merge",
    )(x, oa, ob, oc, g, w_gate, w_br, w_o, post_g)


def _pad_cols(w, width):
    return jnp.pad(w, ((0, 0), (0, width - w.shape[1])))


def _swap_halves(w, group):
    k, n = w.shape
    w = w.reshape(k, n // group, 2, group // 2)
    return w[:, :, ::-1, :].reshape(k, n)


def _pack_layer(P, l):
    w_in = P['w_in'][l]
    (q_lat, kv_lat, k_r, q_b, k_b, v_b, q_c, k_c, v_c, q_i, k_i, w_i, g_a, g_b, g_c) = [
        w_in[:, IN_OFFSETS[i]:IN_OFFSETS[i + 1]] for i in range(len(IN_SPLITS))]
    w_pack = jnp.concatenate(
        [q_lat, kv_lat, _pad_cols(k_r, LANE), _pad_cols(_swap_halves(k_r, A_ROPE), LANE),
         _pad_cols(w_i, LANE), q_b, k_b, v_b, q_c, q_i,
         _pad_cols(k_c, LANE), _pad_cols(v_c, LANE), _pad_cols(k_i, LANE)], axis=1).astype(BF16)
    assert w_pack.shape[1] == _P_END
    uq = P['w_mla_uq'][l].reshape(Q_LORA, H_A, A_NOPE + A_ROPE)
    uq_nope = uq[:, :, :A_NOPE].reshape(Q_LORA, H_A * A_NOPE)
    uq_rope = uq[:, :, A_NOPE:].reshape(Q_LORA, H_A * A_ROPE)
    wuq_pack = jnp.concatenate([uq_nope, uq_rope, _swap_halves(uq_rope, A_ROPE)], axis=1).astype(BF16)
    ukv = P['w_mla_ukv'][l].reshape(KV_LORA, H_A, A_NOPE + A_V)
    wukv_pack = jnp.concatenate([ukv[:, :, :A_NOPE].reshape(KV_LORA, H_A * A_NOPE),
                                 ukv[:, :, A_NOPE:].reshape(KV_LORA, H_A * A_V)], axis=1).astype(BF16)
    return dict(
        w_pack=w_pack, wuq_pack=wuq_pack, wukv_pack=wukv_pack,
        w_gate=jnp.concatenate([g_a, g_b, g_c], axis=1).astype(BF16),
        w_br=P['w_branch'][l].astype(BF16), w_o=P['w_o'][l].astype(BF16),
        mix_pre=P['mix_pre_gain'][l][None], mix_post=P['mix_post_gain'][l][None],
        q_norm=P['mla_q_norm'][l][None], kv_norm=P['mla_kv_norm'][l][None],
        ffn=[dict(pre=P['ffn_pre_gain'][l, i][None], post=P['ffn_post_gain'][l, i][None],
                  w_up=P['w_ffn_up'][l, i].astype(BF16), w_down=P['w_ffn_down'][l, i].astype(BF16))
             for i in range(2)])


def _rope_tables(pos, rows):
    half = A_ROPE // 2
    inv = ROPE_THETA ** (-jnp.arange(half, dtype=F32) / half)
    ang = pos.astype(F32)[:, None] * inv[None, :]
    c, s = jnp.cos(ang), jnp.sin(ang)
    ck = jnp.concatenate([c, c], axis=1)
    sk = jnp.concatenate([-s, s], axis=1)
    rep = max(1, rows // pos.shape[0])
    ck, sk = jnp.tile(ck, (rep, 1)), jnp.tile(sk, (rep, 1))
    return jnp.tile(ck, (1, H_A)), jnp.tile(sk, (1, H_A)), ck, sk


def _heads(a, b, t, h):
    return a.reshape(b, t, h, -1).transpose(0, 2, 1, 3)


def _pad_keys(a, tk):
    pad = [(0, 0)] * a.ndim
    pad[1] = (0, tk - a.shape[1])
    return jnp.pad(a, pad)


def _mixer(x, L, past, rel_bias_tiles, b, t, tm, tq):
    n = b * t
    p_len = 0 if past is None else past[0].shape[1]
    kv_len = p_len + t
    tk = -(-kv_len // KC) * KC
    pos = p_len + jnp.arange(t, dtype=jnp.int32)
    cq, sq, ck, sk = _rope_tables(pos, tm)
    (ckv, krope, kb, vb, kc, vc, ki, wi, qan, qar, qb, qc, qi) = _mixer_in(
        x, L['mix_pre'], L['w_pack'], L['q_norm'], L['wuq_pack'], L['kv_norm'], cq, sq, ck, sk, tm)
    new_rows = (ckv.reshape(b, t, KV_LORA), krope.reshape(b, t, A_ROPE),
                kb.reshape(b, t, H_B, D_B), vb.reshape(b, t, H_B, D_B),
                kc.reshape(b, t, D_C), vc.reshape(b, t, D_C), ki.reshape(b, t, D_IDX))
    if past is None:
        full = new_rows
    else:
        full = tuple(jnp.concatenate([pa, nr], axis=1) for pa, nr in zip(past, new_rows))
    full = tuple(_pad_keys(a, tk) for a in full)
    ckv_all, krope_all, kb_all, vb_all, kc_all, vc_all, ki_all = full

    ukv_tm = tk if (b * tk) % 1024 else 1024
    k_nope, v_a = _ukv(ckv_all.reshape(b * tk, KV_LORA), L['wukv_pack'], ukv_tm)
    q_a = jnp.concatenate([qan.reshape(b, t, H_A, A_NOPE), qar.reshape(b, t, H_A, A_ROPE)],
                          axis=-1).transpose(0, 2, 1, 3)
    k_a = jnp.concatenate(
        [k_nope.reshape(b, tk, H_A, A_NOPE),
         jnp.broadcast_to(krope_all.astype(BF16)[:, :, None, :], (b, tk, H_A, A_ROPE))],
        axis=-1).transpose(0, 2, 1, 3)
    o_a = _mla_attention(q_a, k_a, _heads(v_a, b, tk, H_A), tq, p_len, kv_len)
    o_a = o_a.transpose(0, 2, 1, 3).reshape(n, MIX_W)
    o_b = _sb_attention(_heads(qb, b, t, H_B),
                        kb_all.astype(BF16).transpose(0, 2, 1, 3), vb_all.astype(BF16).transpose(0, 2, 1, 3),
                        tq, p_len, kv_len)
    o_b = o_b.transpose(0, 2, 1, 3).reshape(n, MIX_W)
    nq = t // tq

    def stack(a, d):
        return a.reshape(b, nq, tq, H_C, d).transpose(0, 1, 3, 2, 4).reshape(b, nq, H_C * tq, d)

    o_c = _dsa_attention(stack(qc, D_C), stack(qi, D_IDX), stack(wi, 1),
                         kc_all.astype(BF16), vc_all.astype(BF16), ki_all.astype(BF16),
                         rel_bias_tiles, tq, p_len, kv_len)
    o_c = o_c.reshape(b, nq, H_C, tq, D_C).transpose(0, 1, 3, 2, 4).reshape(n, MIX_W)
    x = _merge(x, o_a, o_b, o_c, L['mix_pre'], L['w_gate'], L['w_br'], L['w_o'], L['mix_post'], tm)
    return x, new_rows


def _trunk(x, past, layers, rel_bias):
    b, t, _ = x.shape
    n = b * t
    tm = min(512, n)
    tq = min(128, t)
    p_len = 0 if past is None else past[0][0].shape[1]
    n_off = (p_len + t - tq) // KC + 1
    bias_tiles = _bias_tiles(rel_bias, tq, n_off)
    x = x.reshape(n, D_MODEL)
    rows = []
    for l, L in enumerate(layers):
        f = L['ffn'][0]
        x = _ffn_half(x, f['pre'], f['w_up'], f['w_down'], f['post'], tm)
        x, new = _mixer(x, L, None if past is None else past[l], bias_tiles, b, t, tm, tq)
        f = L['ffn'][1]
        x = _ffn_half(x, f['pre'], f['w_up'], f['w_down'], f['post'], tm)
        rows.append(new)
    return x.reshape(b, t, D_MODEL), tuple(jnp.stack(g) for g in zip(*rows))


def kernel(x_prompt, x_sample, cache_mla_ckv, cache_mla_krope, cache_sb_k, cache_sb_v, cache_dsa_k, cache_dsa_v, cache_dsa_kidx, rel_bias, ffn_pre_gain, w_ffn_up, w_ffn_down, ffn_post_gain, mix_pre_gain, w_in, mla_q_norm, w_mla_uq, mla_kv_norm, w_mla_ukv, w_branch, w_o, mix_post_gain):
    P = dict(ffn_pre_gain=ffn_pre_gain, w_ffn_up=w_ffn_up, w_ffn_down=w_ffn_down,
             ffn_post_gain=ffn_post_gain, mix_pre_gain=mix_pre_gain, w_in=w_in,
             mla_q_norm=mla_q_norm, w_mla_uq=w_mla_uq, mla_kv_norm=mla_kv_norm,
             w_mla_ukv=w_mla_ukv, w_branch=w_branch, w_o=w_o, mix_post_gain=mix_post_gain)
    depth = w_in.shape[0]
    layers = [_pack_layer(P, l) for l in range(depth)]
    y_prompt, p_rows = _trunk(x_prompt, None, layers, rel_bias)
    caches = (cache_mla_ckv, cache_mla_krope, cache_sb_k, cache_sb_v,
              cache_dsa_k, cache_dsa_v, cache_dsa_kidx)
    past = [tuple(c[l] for c in caches) for l in range(depth)]
    y_sample, s_rows = _trunk(x_sample, past, layers, rel_bias)
    return (y_prompt, y_sample) + p_rows + s_rows
```

```python
import functools
import math

import numpy as np
import jax
import jax.numpy as jnp
from jax import lax
from jax.experimental import pallas as pl
from jax.experimental.pallas import tpu as pltpu

D_MODEL = 1024
CHUNK = 64
CHUNK_SHIFT = 6
EPS = 1e-6
H_A, A_NOPE, A_ROPE, A_V = 8, 64, 32, 64
Q_LORA, KV_LORA = 384, 256
ROPE_THETA = 10000.0
H_B, D_B = 8, 64
H_C, D_C = 8, 64
H_IDX, D_IDX = 8, 64
TOPK_MAX = 256
NUM_BUCKETS, MAX_DISTANCE = 32, 256
MIX_W = 512
D_FF = 2816
IN_SPLITS = (Q_LORA, KV_LORA, A_ROPE, H_B * D_B, H_B * D_B, H_B * D_B,
             H_C * D_C, D_C, D_C, H_IDX * D_IDX, D_IDX, H_IDX,
             D_MODEL, D_MODEL, D_MODEL)
IN_OFFSETS = tuple(int(o) for o in np.cumsum((0,) + IN_SPLITS))

LANE = 128
KC = 128
TQ = 128
FF_CHUNK = 256
NEG = -1e30
INT_MIN = -2 ** 31
VMEM_LIMIT = 56 * 1024 * 1024
BF16 = jnp.bfloat16
F32 = jnp.float32


def _cparams(*sem):
    return pltpu.CompilerParams(dimension_semantics=sem, vmem_limit_bytes=VMEM_LIMIT)


def _const_spec(shape):
    nd = len(shape)
    return pl.BlockSpec(shape, lambda *_: (0,) * nd, pipeline_mode=pl.Buffered(1))


def _rms(x, g):
    return x * lax.rsqrt(jnp.mean(x * x, axis=-1, keepdims=True) + EPS) * g


def _dot(a, b):
    return jnp.dot(a, b, preferred_element_type=F32)


def _dot_nt(a, b):
    return lax.dot_general(a, b, (((1,), (1,)), ((), ())), preferred_element_type=F32)


def _ffn_kernel(x_ref, pre_ref, wup_ref, wdn_ref, post_ref, o_ref):
    x = x_ref[...]
    h = _rms(x, pre_ref[...]).astype(BF16)
    acc = jnp.zeros(x.shape, F32)
    for c in range(D_FF // FF_CHUNK):
        lo = c * FF_CHUNK
        gate = _dot(h, wup_ref[:, lo:lo + FF_CHUNK])
        up = _dot(h, wup_ref[:, D_FF + lo:D_FF + lo + FF_CHUNK])
        act = (gate * jax.nn.sigmoid(gate) * up).astype(BF16)
        acc = acc + _dot(act, wdn_ref[lo:lo + FF_CHUNK, :])
    o_ref[...] = x + 0.5 * _rms(acc, post_ref[...])


def _ffn_half(x, pre_g, w_up, w_down, post_g, tm):
    n = x.shape[0]
    row = pl.BlockSpec((tm, D_MODEL), lambda i: (i, 0))
    return pl.pallas_call(
        _ffn_kernel,
        grid=(n // tm,),
        in_specs=[row, _const_spec((1, D_MODEL)), _const_spec((D_MODEL, 2 * D_FF)),
                  _const_spec((D_FF, D_MODEL)), _const_spec((1, D_MODEL))],
        out_specs=row,
        out_shape=jax.ShapeDtypeStruct((n, D_MODEL), F32),
        compiler_params=_cparams("parallel"),
        name="ffn_half",
    )(x, pre_g, w_up, w_down, post_g)


_P_QLAT = 0
_P_KVLAT = _P_QLAT + Q_LORA
_P_KR = _P_KVLAT + KV_LORA
_P_KRSW = _P_KR + LANE
_P_WI = _P_KRSW + LANE
_P_QB = _P_WI + LANE
_P_KB = _P_QB + MIX_W
_P_VB = _P_KB + MIX_W
_P_QC = _P_VB + MIX_W
_P_QI = _P_QC + MIX_W
_P_KC = _P_QI + MIX_W
_P_VC = _P_KC + LANE
_P_KI = _P_VC + LANE
_P_END = _P_KI + LANE


def _in_kernel(x_ref, g_ref, w_ref, qn_ref, wuq_ref, kvn_ref, cq_ref, sq_ref, ck_ref, sk_ref,
               ckv_o, krope_o, kb_o, vb_o, kc_o, vc_o, ki_o, wi_o,
               qan_o, qar_o, qb_o, qc_o, qi_o):
    u = _rms(x_ref[...], g_ref[...]).astype(BF16)
    p = _dot(u, w_ref[...])
    c_q = _rms(p[:, _P_QLAT:_P_QLAT + Q_LORA], qn_ref[...]).astype(BF16)
    qa = _dot(c_q, wuq_ref[...])
    n_nope, n_rope = H_A * A_NOPE, H_A * A_ROPE
    scale_a = (A_NOPE + A_ROPE) ** -0.5
    qan_o[...] = (qa[:, :n_nope] * scale_a).astype(BF16)
    q_rot = qa[:, n_nope:n_nope + n_rope] * cq_ref[...] + qa[:, n_nope + n_rope:] * sq_ref[...]
    qar_o[...] = (q_rot * scale_a).astype(BF16)
    ckv_o[...] = _rms(p[:, _P_KVLAT:_P_KVLAT + KV_LORA], kvn_ref[...])
    krope_o[...] = (p[:, _P_KR:_P_KR + A_ROPE] * ck_ref[...]
                    + p[:, _P_KRSW:_P_KRSW + A_ROPE] * sk_ref[...])
    qb_o[...] = (p[:, _P_QB:_P_QB + MIX_W] * D_B ** -0.5).astype(BF16)
    kb_o[...] = p[:, _P_KB:_P_KB + MIX_W]
    vb_o[...] = p[:, _P_VB:_P_VB + MIX_W]
    qc_o[...] = (p[:, _P_QC:_P_QC + MIX_W] * D_C ** -0.5).astype(BF16)
    qi_o[...] = (p[:, _P_QI:_P_QI + MIX_W] * D_IDX ** -0.5).astype(BF16)
    kc_o[...] = p[:, _P_KC:_P_KC + D_C]
    vc_o[...] = p[:, _P_VC:_P_VC + D_C]
    ki_o[...] = p[:, _P_KI:_P_KI + D_IDX]
    wi_o[...] = p[:, _P_WI:_P_WI + H_IDX] * H_IDX ** -0.5


def _mixer_in(x, g, w_pack, qn, wuq_pack, kvn, cq, sq, ck, sk, tm):
    n = x.shape[0]
    n_tab = cq.shape[0] // tm

    def row(w):
        return pl.BlockSpec((tm, w), lambda i: (i, 0))

    def tab(w):
        return pl.BlockSpec((tm, w), lambda i: (i % n_tab, 0))

    widths_f32 = (KV_LORA, A_ROPE, MIX_W, MIX_W, D_C, D_C, D_IDX, H_IDX)
    widths_bf16 = (H_A * A_NOPE, H_A * A_ROPE, MIX_W, MIX_W, MIX_W)
    return pl.pallas_call(
        _in_kernel,
        grid=(n // tm,),
        in_specs=[row(D_MODEL), _const_spec((1, D_MODEL)), _const_spec(w_pack.shape),
                  _const_spec((1, Q_LORA)), _const_spec(wuq_pack.shape), _const_spec((1, KV_LORA)),
                  tab(H_A * A_ROPE), tab(H_A * A_ROPE), tab(A_ROPE), tab(A_ROPE)],
        out_specs=[row(w) for w in widths_f32 + widths_bf16],
        out_shape=([jax.ShapeDtypeStruct((n, w), F32) for w in widths_f32]
                   + [jax.ShapeDtypeStruct((n, w), BF16) for w in widths_bf16]),
        compiler_params=_cparams("parallel"),
        name="mixer_in",
    )(x, g, w_pack, qn, wuq_pack, kvn, cq, sq, ck, sk)


def _ukv_kernel(c_ref, w_ref, k_o, v_o):
    kv = _dot(c_ref[...].astype(BF16), w_ref[...])
    k_o[...] = kv[:, :H_A * A_NOPE].astype(BF16)
    v_o[...] = kv[:, H_A * A_NOPE:].astype(BF16)


def _ukv(ckv, w_pack, tm):
    n = ckv.shape[0]
    return pl.pallas_call(
        _ukv_kernel,
        grid=(n // tm,),
        in_specs=[pl.BlockSpec((tm, KV_LORA), lambda i: (i, 0)), _const_spec(w_pack.shape)],
        out_specs=[pl.BlockSpec((tm, H_A * A_NOPE), lambda i: (i, 0)),
                   pl.BlockSpec((tm, H_A * A_V), lambda i: (i, 0))],
        out_shape=[jax.ShapeDtypeStruct((n, H_A * A_NOPE), BF16),
                   jax.ShapeDtypeStruct((n, H_A * A_V), BF16)],
        compiler_params=_cparams("parallel"),
        name="mla_ukv",
    )(ckv, w_pack)


def _num_key_steps(q0, kv_len):
    k_end = jnp.minimum(kv_len, ((q0 + TQ - 1) // CHUNK + 1) * CHUNK)
    return (k_end + KC - 1) // KC


def _positions(q0, k0):
    kpos = k0 + lax.broadcasted_iota(jnp.int32, (KC, TQ), 0)
    qpos = q0 + lax.broadcasted_iota(jnp.int32, (KC, TQ), 1)
    return kpos, qpos


def _chunk_visible(kpos, qpos, kv_len, tk):
    vis = (kpos >> CHUNK_SHIFT) <= (qpos >> CHUNK_SHIFT)
    return vis & (kpos < kv_len) if kv_len < tk else vis


def _head_lanes(x, width, idx):
    lane = lax.broadcasted_iota(jnp.int32, x.shape, 1) & (LANE - 1)
    return jnp.where((lane >> (width.bit_length() - 1)) == idx, x.astype(F32), 0.0).astype(x.dtype)


def _fill_transposed(src_ref, dst_ref):
    tk, width = src_ref.shape[1], src_ref.shape[2]

    def body(c, carry):
        k0 = pl.multiple_of(c * KC, KC)
        blk = src_ref[0, pl.ds(k0, KC), :].astype(F32)
        for g in range(width // LANE):
            dst_ref[g * LANE:(g + 1) * LANE, pl.ds(k0, KC)] = blk[:, g * LANE:(g + 1) * LANE].T.astype(dst_ref.dtype)
        return carry

    lax.fori_loop(0, tk // KC, body, 0)


def _store_heads_token_major(o_ref, heads):
    for g in range(len(heads) // 2):
        pair = jnp.concatenate([heads[2 * g], heads[2 * g + 1]], axis=0)
        o_ref[0, :, g * LANE:(g + 1) * LANE] = pair.T.astype(o_ref.dtype)


def _attn_call(kernel_fn, name, q_arrays, k_arrays, scratch, extra_in=(), extra_specs=()):
    b, t = q_arrays[0].shape[:2]
    in_specs = ([pl.BlockSpec((1, TQ, a.shape[2]), lambda i, j: (i, j, 0)) for a in q_arrays]
                + [pl.BlockSpec((1,) + a.shape[1:], lambda i, j: (i, 0, 0)) for a in k_arrays]
                + list(extra_specs))
    return pl.pallas_call(
        kernel_fn,
        grid=(b, t // TQ),
        in_specs=in_specs,
        out_specs=pl.BlockSpec((1, TQ, MIX_W), lambda i, j: (i, j, 0)),
        out_shape=jax.ShapeDtypeStruct((b, t, MIX_W), BF16),
        scratch_shapes=scratch,
        compiler_params=_cparams("parallel", "arbitrary"),
        name=name,
    )(*q_arrays, *k_arrays, *extra_in)


def _mla_kernel(qn_ref, qr_ref, kn_ref, kr_ref, v_ref, o_ref, qcat_ref, vt_ref, m_ref, l_ref, acc_ref,
                *, q_off, kv_len):
    tk = kn_ref.shape[1]
    qb = pl.program_id(1)
    q0 = q_off + qb * TQ
    nk = _num_key_steps(q0, kv_len)

    @pl.when(qb == 0)
    def _():
        _fill_transposed(v_ref, vt_ref)

    for h in range(H_A):
        qcat_ref[h, :, :LANE] = _head_lanes(qn_ref[0, :, (h // 2) * LANE:(h // 2 + 1) * LANE], A_NOPE, h % 2)
        qcat_ref[h, :, LANE:] = _head_lanes(qr_ref[0, :, (h // 4) * LANE:(h // 4 + 1) * LANE], A_ROPE, h % 4)
    m_ref[...] = jnp.full(m_ref.shape, NEG, F32)
    l_ref[...] = jnp.zeros(l_ref.shape, F32)
    acc_ref[...] = jnp.zeros(acc_ref.shape, F32)

    def body(c, carry):
        k0 = pl.multiple_of(c * KC, KC)
        vis = _chunk_visible(*_positions(q0, k0), kv_len, tk)
        kr = kr_ref[0, pl.ds(k0, KC), :]
        heads = range(H_A)
        s = [_dot_nt(jnp.concatenate([kn_ref[0, pl.ds(k0, KC), (h // 2) * LANE:(h // 2 + 1) * LANE], kr], axis=1),
                     qcat_ref[h]) for h in heads]
        s = [jnp.where(vis, s[h], NEG) for h in heads]
        m_prev = [m_ref[h] for h in heads]
        m_new = [jnp.maximum(m_prev[h], jnp.max(s[h], axis=0, keepdims=True)) for h in heads]
        p = [jnp.exp(s[h] - m_new[h]) for h in heads]
        alpha = [jnp.exp(m_prev[h] - m_new[h]) for h in heads]
        pv = [_dot(vt_ref[h * A_V:(h + 1) * A_V, pl.ds(k0, KC)], p[h].astype(BF16)) for h in heads]
        for h in heads:
            l_ref[h] = alpha[h] * l_ref[h] + jnp.sum(p[h], axis=0, keepdims=True)
            acc_ref[h] = alpha[h] * acc_ref[h] + pv[h]
            m_ref[h] = m_new[h]
        return carry

    lax.fori_loop(0, nk, body, 0)
    _store_heads_token_major(o_ref, [acc_ref[h] / l_ref[h] for h in range(H_A)])


def _mla_attention(qn, qr, kn, kr4, v, q_off, kv_len):
    tk = kn.shape[1]
    return _attn_call(
        functools.partial(_mla_kernel, q_off=q_off, kv_len=kv_len), "mla_attention",
        (qn, qr), (kn, kr4, v),
        [pltpu.VMEM((H_A, TQ, 2 * LANE), BF16), pltpu.VMEM((H_A * A_V, tk), BF16),
         pltpu.VMEM((H_A, 1, TQ), F32), pltpu.VMEM((H_A, 1, TQ), F32), pltpu.VMEM((H_A, A_V, TQ), F32)])


def _sb_kernel(q_ref, k_ref, v_ref, o_ref, qm_ref, vt_ref, run_ref, acc_ref, *, q_off, kv_len):
    qb = pl.program_id(1)
    q0 = q_off + qb * TQ
    nk = _num_key_steps(q0, kv_len)

    @pl.when(qb == 0)
    def _():
        _fill_transposed(v_ref, vt_ref)

    for h in range(H_B):
        qm_ref[h] = _head_lanes(q_ref[0, :, (h // 2) * LANE:(h // 2 + 1) * LANE], D_B, h % 2)
    run_ref[...] = jnp.zeros(run_ref.shape, F32)
    acc_ref[...] = jnp.zeros(acc_ref.shape, F32)
    later = jnp.where(lax.broadcasted_iota(jnp.int32, (KC, KC), 1)
                      > lax.broadcasted_iota(jnp.int32, (KC, KC), 0), 1.0, 0.0).astype(BF16)

    def body(i, carry):
        k0 = pl.multiple_of((nk - 1 - i) * KC, KC)
        kpos, qpos = _positions(q0, k0)
        strict = kpos < qpos
        heads = range(H_B)
        z = [_dot_nt(k_ref[0, pl.ds(k0, KC), (h // 2) * LANE:(h // 2 + 1) * LANE], qm_ref[h]) for h in heads]
        soft = [jnp.log(1.0 + jnp.exp(-jnp.abs(z[h]))) for h in heads]
        log_keep = [jnp.where(strict, jnp.minimum(-z[h], 0.0) - soft[h], 0.0) for h in heads]
        hi = [log_keep[h].astype(BF16) for h in heads]
        lo = [(log_keep[h] - hi[h].astype(F32)).astype(BF16) for h in heads]
        cum = [_dot(later, jnp.concatenate([hi[h], lo[h]], axis=1)) for h in heads]
        stick = [cum[h][:, :TQ] + cum[h][:, TQ:] + run_ref[h] for h in heads]
        a = [jnp.where(strict, jnp.exp(jnp.minimum(z[h], 0.0) - soft[h] + stick[h]), 0.0) for h in heads]
        av = [_dot(vt_ref[h * D_B:(h + 1) * D_B, pl.ds(k0, KC)], a[h].astype(BF16)) for h in heads]
        for h in heads:
            acc_ref[h] = acc_ref[h] + av[h]
            run_ref[h] = run_ref[h] + jnp.sum(log_keep[h], axis=0, keepdims=True)
        return carry

    lax.fori_loop(0, nk, body, 0)
    _store_heads_token_major(o_ref, [acc_ref[h] for h in range(H_B)])


def _sb_attention(q, k, v, q_off, kv_len):
    tk = k.shape[1]
    return _attn_call(
        functools.partial(_sb_kernel, q_off=q_off, kv_len=kv_len), "sb_attention",
        (q,), (k, v),
        [pltpu.VMEM((H_B, TQ, LANE), BF16), pltpu.VMEM((H_B * D_B, tk), BF16),
         pltpu.VMEM((H_B, 1, TQ), F32), pltpu.VMEM((H_B, D_B, TQ), F32)])


def _t5_bucket_np(rel):
    nb = NUM_BUCKETS // 2
    max_exact = nb // 2
    ret = np.where(rel > 0, nb, 0)
    n = np.abs(rel)
    nf = np.maximum(n, 1).astype(np.float32)
    large = max_exact + (np.log(nf / max_exact) / math.log(MAX_DISTANCE / max_exact)
                         * (nb - max_exact)).astype(np.int32)
    large = np.minimum(large, nb - 1)
    return (ret + np.where(n < max_exact, n, large)).astype(np.int32)


def _bias_kernel(rb_ref, bkt_ref, o_ref):
    bkt = bkt_ref[0]
    for h in range(H_C):
        acc = jnp.zeros(bkt.shape, F32)
        for b in range(NUM_BUCKETS):
            acc = jnp.where(bkt == b, rb_ref[b, h], acc)
        o_ref[0, :, h * TQ:(h + 1) * TQ] = acc


def _bias_tiles(rel_bias, n_off):
    j = np.arange(KC)[None, :, None]
    i = np.arange(TQ)[None, None, :]
    d = np.arange(n_off)[:, None, None]
    buckets = jnp.asarray(_t5_bucket_np(j - i - d * KC))
    return pl.pallas_call(
        _bias_kernel,
        grid=(n_off,),
        in_specs=[pl.BlockSpec(memory_space=pltpu.SMEM),
                  pl.BlockSpec((1, KC, TQ), lambda d: (d, 0, 0))],
        out_specs=pl.BlockSpec((1, KC, H_C * TQ), lambda d: (d, 0, 0)),
        out_shape=jax.ShapeDtypeStruct((n_off, KC, H_C * TQ), F32),
        compiler_params=_cparams("parallel"),
        name="dsa_bias_tiles",
    )(rel_bias, buckets)


def _dsa_kernel(q_ref, qi_ref, k_ref, ki_ref, v_ref, wi_ref, bias_ref, o_ref,
                qm_ref, qim_ref, vt_ref, key_ref, m_ref, l_ref, acc_ref, *, q_off, kv_len):
    tk = k_ref.shape[1]
    qb = pl.program_id(1)
    q0 = q_off + qb * TQ
    nk = _num_key_steps(q0, kv_len)
    topk = min(TOPK_MAX, kv_len // 4)

    @pl.when(qb == 0)
    def _():
        _fill_transposed(v_ref, vt_ref)

    for h in range(H_C):
        blk = slice((h // 2) * LANE, (h // 2 + 1) * LANE)
        qm_ref[h * TQ:(h + 1) * TQ, :] = _head_lanes(q_ref[0, :, blk], D_C, h % 2)
        qim_ref[h * TQ:(h + 1) * TQ, :] = _head_lanes(qi_ref[0, :, blk], D_IDX, h % 2)

    wi = wi_ref[0]

    def score_body(c, carry):
        k0 = pl.multiple_of(c * KC, KC)
        vis = _chunk_visible(*_positions(q0, k0), kv_len, tk)
        r = _dot_nt(ki_ref[0, pl.ds(k0, KC), :], qim_ref[...])
        score = jnp.zeros((KC, TQ), F32)
        for h in range(H_IDX):
            score = score + jnp.maximum(r[:, h * TQ:(h + 1) * TQ], 0.0) * wi[h:h + 1, :]
        bits = lax.bitcast_convert_type(score + 0.0, jnp.int32)
        key = bits ^ ((bits >> 31) & 0x7FFFFFFF)
        key_ref[pl.ds(k0, KC), :] = jnp.where(vis, key, INT_MIN)
        return carry

    lax.fori_loop(0, nk, score_body, 0)

    def count(pred):
        def body(c, acc):
            k0 = pl.multiple_of(c * KC, KC)
            return acc + jnp.where(pred(key_ref[pl.ds(k0, KC), :]), 1, 0)
        acc = lax.fori_loop(0, nk, body, jnp.zeros((KC, TQ), jnp.int32))
        return jnp.sum(acc, axis=0, keepdims=True)

    def search(i, thr):
        cand = thr + jnp.left_shift(jnp.int32(1), 31 - i)
        cnt = count(lambda key: key >= cand)
        return jnp.where(cnt >= topk, cand, thr)

    thr = lax.fori_loop(0, 32, search, jnp.full((1, TQ), INT_MIN, jnp.int32))

    n_ge = count(lambda key: key >= thr)
    surplus = jnp.where(thr > INT_MIN, n_ge - topk, 0)

    @pl.when(jnp.max(surplus) > 0)
    def _():
        need = topk - count(lambda key: key > thr)
        upto = jnp.where(lax.broadcasted_iota(jnp.int32, (KC, KC), 1)
                         <= lax.broadcasted_iota(jnp.int32, (KC, KC), 0), 1.0, 0.0).astype(BF16)

        def body(c, seen):
            k0 = pl.multiple_of(c * KC, KC)
            key = key_ref[pl.ds(k0, KC), :]
            tie = key == thr
            rank = seen + _dot(upto, jnp.where(tie, 1.0, 0.0).astype(BF16)).astype(jnp.int32)
            key_ref[pl.ds(k0, KC), :] = jnp.where(tie & (rank > need), INT_MIN, key)
            return seen + jnp.sum(jnp.where(tie, 1, 0), axis=0, keepdims=True)

        lax.fori_loop(0, nk, body, jnp.zeros((1, TQ), jnp.int32))

    thr_sel = jnp.maximum(thr, INT_MIN + 1)
    m_ref[...] = jnp.full(m_ref.shape, NEG, F32)
    l_ref[...] = jnp.zeros(l_ref.shape, F32)
    acc_ref[...] = jnp.zeros(acc_ref.shape, F32)
    d0 = q0 // KC

    def attend(c, carry):
        k0 = pl.multiple_of(c * KC, KC)
        off = jnp.where(key_ref[pl.ds(k0, KC), :] >= thr_sel, 0.0, NEG)
        s = (_dot_nt(k_ref[0, pl.ds(k0, KC), :], qm_ref[...]) + bias_ref[d0 - c]
             + jnp.concatenate([off] * H_C, axis=1))
        m_prev = m_ref[...]
        m_new = jnp.maximum(m_prev, jnp.max(s, axis=0, keepdims=True))
        alpha = jnp.exp(m_prev - m_new)
        p = jnp.exp(s - m_new)
        l_ref[...] = alpha * l_ref[...] + jnp.sum(p, axis=0, keepdims=True)
        acc_ref[...] = alpha * acc_ref[...] + _dot(vt_ref[:D_C, pl.ds(k0, KC)], p.astype(BF16))
        m_ref[...] = m_new
        return carry

    lax.fori_loop(0, nk, attend, 0)
    o = acc_ref[...] / l_ref[...]
    _store_heads_token_major(o_ref, [o[:, h * TQ:(h + 1) * TQ] for h in range(H_C)])


def _dsa_attention(q, qi, k2, ki2, v2, wi_t, bias, q_off, kv_len):
    tk = k2.shape[1]
    hq = H_C * TQ
    return _attn_call(
        functools.partial(_dsa_kernel, q_off=q_off, kv_len=kv_len), "dsa_attention",
        (q, qi), (k2, ki2, v2),
        [pltpu.VMEM((hq, LANE), BF16), pltpu.VMEM((hq, LANE), BF16), pltpu.VMEM((LANE, tk), BF16),
         pltpu.VMEM((tk, TQ), jnp.int32), pltpu.VMEM((1, hq), F32), pltpu.VMEM((1, hq), F32),
         pltpu.VMEM((D_C, hq), F32)],
        extra_in=(wi_t, bias),
        extra_specs=(pl.BlockSpec((1, H_IDX, TQ), lambda i, j: (i, 0, j)), _const_spec(bias.shape)))


def _merge_kernel(x_ref, oa_ref, ob_ref, oc_ref, g_ref, wg_ref, wbr_ref, wo_ref, post_ref, o_ref):
    x = x_ref[...]
    u = _rms(x, g_ref[...]).astype(BF16)
    merged = jnp.zeros(x.shape, F32)
    for i, o_br in enumerate((oa_ref, ob_ref, oc_ref)):
        gate = jax.nn.sigmoid(_dot(u, wg_ref[:, i * D_MODEL:(i + 1) * D_MODEL]))
        merged = merged + gate * _dot(o_br[...], wbr_ref[i])
    y = _dot(merged.astype(BF16), wo_ref[...])
    o_ref[...] = x + _rms(y, post_ref[...])


def _merge(x, oa, ob, oc, g, w_gate, w_br, w_o, post_g, tm):
    n = x.shape[0]
    row = pl.BlockSpec((tm, D_MODEL), lambda i: (i, 0))
    br = pl.BlockSpec((tm, MIX_W), lambda i: (i, 0))
    return pl.pallas_call(
        _merge_kernel,
        grid=(n // tm,),
        in_specs=[row, br, br, br, _const_spec((1, D_MODEL)), _const_spec(w_gate.shape),
                  _const_spec(w_br.shape), _const_spec(w_o.shape), _const_spec((1, D_MODEL))],
        out_specs=row,
        out_shape=jax.ShapeDtypeStruct((n, D_MODEL), F32),
        compiler_params=_cparams("parallel"),
        name="mixer_merge",
    )(x, oa, ob, oc, g, w_gate, w_br, w_o, post_g)


def _pad_cols(w, width):
    return jnp.pad(w, ((0, 0), (0, width - w.shape[1])))


def _swap_halves(w, group):
    k, n = w.shape
    w = w.reshape(k, n // group, 2, group // 2)
    return w[:, :, ::-1, :].reshape(k, n)


def _pack_layer(P, l):
    w_in = P['w_in'][l]
    (q_lat, kv_lat, k_r, q_b, k_b, v_b, q_c, k_c, v_c, q_i, k_i, w_i, g_a, g_b, g_c) = [
        w_in[:, IN_OFFSETS[i]:IN_OFFSETS[i + 1]] for i in range(len(IN_SPLITS))]
    w_pack = jnp.concatenate(
        [q_lat, kv_lat, _pad_cols(k_r, LANE), _pad_cols(_swap_halves(k_r, A_ROPE), LANE),
         _pad_cols(w_i, LANE), q_b, k_b, v_b, q_c, q_i,
         _pad_cols(k_c, LANE), _pad_cols(v_c, LANE), _pad_cols(k_i, LANE)], axis=1).astype(BF16)
    assert w_pack.shape[1] == _P_END
    uq = P['w_mla_uq'][l].reshape(Q_LORA, H_A, A_NOPE + A_ROPE)
    uq_nope = uq[:, :, :A_NOPE].reshape(Q_LORA, H_A * A_NOPE)
    uq_rope = uq[:, :, A_NOPE:].reshape(Q_LORA, H_A * A_ROPE)
    wuq_pack = jnp.concatenate([uq_nope, uq_rope, _swap_halves(uq_rope, A_ROPE)], axis=1).astype(BF16)
    ukv = P['w_mla_ukv'][l].reshape(KV_LORA, H_A, A_NOPE + A_V)
    wukv_pack = jnp.concatenate([ukv[:, :, :A_NOPE].reshape(KV_LORA, H_A * A_NOPE),
                                 ukv[:, :, A_NOPE:].reshape(KV_LORA, H_A * A_V)], axis=1).astype(BF16)
    return dict(
        w_pack=w_pack, wuq_pack=wuq_pack, wukv_pack=wukv_pack,
        w_gate=jnp.concatenate([g_a, g_b, g_c], axis=1).astype(BF16),
        w_br=P['w_branch'][l].astype(BF16), w_o=P['w_o'][l].astype(BF16),
        mix_pre=P['mix_pre_gain'][l][None], mix_post=P['mix_post_gain'][l][None],
        q_norm=P['mla_q_norm'][l][None], kv_norm=P['mla_kv_norm'][l][None],
        ffn=[dict(pre=P['ffn_pre_gain'][l, i][None], post=P['ffn_post_gain'][l, i][None],
                  w_up=P['w_ffn_up'][l, i].astype(BF16), w_down=P['w_ffn_down'][l, i].astype(BF16))
             for i in range(2)])


def _rope_tables(pos, rows):
    half = A_ROPE // 2
    inv = ROPE_THETA ** (-jnp.arange(half, dtype=F32) / half)
    ang = pos.astype(F32)[:, None] * inv[None, :]
    c, s = jnp.cos(ang), jnp.sin(ang)
    ck = jnp.concatenate([c, c], axis=1)
    sk = jnp.concatenate([-s, s], axis=1)
    rep = max(1, rows // pos.shape[0])
    ck, sk = jnp.tile(ck, (rep, 1)), jnp.tile(sk, (rep, 1))
    return jnp.tile(ck, (1, H_A)), jnp.tile(sk, (1, H_A)), ck, sk


def _pad_axis1(a, size):
    if a.shape[1] == size:
        return a
    pad = [(0, 0)] * a.ndim
    pad[1] = (0, size - a.shape[1])
    return jnp.pad(a, pad)


def _mixer(x, L, past, bias_tiles, b, t, tm):
    n = b * t
    p_len = 0 if past is None else past[0].shape[1]
    kv_len = p_len + t
    tk = -(-kv_len // KC) * KC
    tqp = -(-t // TQ) * TQ
    pos = p_len + jnp.arange(t, dtype=jnp.int32)
    cq, sq, ck, sk = _rope_tables(pos, tm)
    (ckv, krope, kb, vb, kc, vc, ki, wi, qan, qar, qb, qc, qi) = _mixer_in(
        x, L['mix_pre'], L['w_pack'], L['q_norm'], L['wuq_pack'], L['kv_norm'], cq, sq, ck, sk, tm)
    new_rows = (ckv.reshape(b, t, KV_LORA), krope.reshape(b, t, A_ROPE),
                kb.reshape(b, t, H_B, D_B), vb.reshape(b, t, H_B, D_B),
                kc.reshape(b, t, D_C), vc.reshape(b, t, D_C), ki.reshape(b, t, D_IDX))
    flat = (new_rows[0], new_rows[1], kb.reshape(b, t, MIX_W), vb.reshape(b, t, MIX_W)) + new_rows[4:]
    if past is not None:
        past_flat = (past[0], past[1], past[2].reshape(b, p_len, MIX_W), past[3].reshape(b, p_len, MIX_W)) + past[4:]
        flat = tuple(jnp.concatenate([pa, nr], axis=1) for pa, nr in zip(past_flat, flat))
    ckv_all, krope_all, kb_all, vb_all, kc_all, vc_all, ki_all = (_pad_axis1(a, tk) for a in flat)

    def queries(a):
        return _pad_axis1(a.reshape(b, t, -1), tqp)

    def attended(o):
        return o[:, :t].reshape(n, MIX_W)

    def twice(a):
        return jnp.tile(a.astype(BF16), (1, 1, 2))

    k_nope, v_a = _ukv(ckv_all.reshape(b * tk, KV_LORA), L['wukv_pack'], 1024 if (b * tk) % 1024 == 0 else tk)
    o_a = _mla_attention(queries(qan), queries(qar), k_nope.reshape(b, tk, MIX_W),
                         jnp.tile(krope_all.astype(BF16), (1, 1, LANE // A_ROPE)),
                         v_a.reshape(b, tk, MIX_W), p_len, kv_len)
    o_b = _sb_attention(queries(qb), kb_all.astype(BF16), vb_all.astype(BF16), p_len, kv_len)
    wi_t = _pad_axis1(wi.reshape(b, t, H_IDX), tqp).transpose(0, 2, 1)
    o_c = _dsa_attention(queries(qc), queries(qi), twice(kc_all), twice(ki_all), twice(vc_all), wi_t,
                         bias_tiles, p_len, kv_len)
    x = _merge(x, attended(o_a), attended(o_b), attended(o_c), L['mix_pre'], L['w_gate'], L['w_br'], L['w_o'],
               L['mix_post'], tm)
    return x, new_rows


def _trunk(x, past, layers, rel_bias):
    b, t, _ = x.shape
    n = b * t
    tm = min(512, n)
    p_len = 0 if past is None else past[0][0].shape[1]
    assert p_len % KC == 0
    n_off = (p_len + (-(-t // TQ) - 1) * TQ) // KC + 1
    bias_tiles = _bias_tiles(rel_bias, n_off)
    x = x.reshape(n, D_MODEL)
    rows = []
    for l, L in enumerate(layers):
        f = L['ffn'][0]
        x = _ffn_half(x, f['pre'], f['w_up'], f['w_down'], f['post'], tm)
        x, new = _mixer(x, L, None if past is None else past[l], bias_tiles, b, t, tm)
        f = L['ffn'][1]
        x = _ffn_half(x, f['pre'], f['w_up'], f['w_down'], f['post'], tm)
        rows.append(new)
    return x.reshape(b, t, D_MODEL), tuple(jnp.stack(g) for g in zip(*rows))


def kernel(x_prompt, x_sample, cache_mla_ckv, cache_mla_krope, cache_sb_k, cache_sb_v, cache_dsa_k, cache_dsa_v, cache_dsa_kidx, rel_bias, ffn_pre_gain, w_ffn_up, w_ffn_down, ffn_post_gain, mix_pre_gain, w_in, mla_q_norm, w_mla_uq, mla_kv_norm, w_mla_ukv, w_branch, w_o, mix_post_gain):
    P = dict(ffn_pre_gain=ffn_pre_gain, w_ffn_up=w_ffn_up, w_ffn_down=w_ffn_down,
             ffn_post_gain=ffn_post_gain, mix_pre_gain=mix_pre_gain, w_in=w_in,
             mla_q_norm=mla_q_norm, w_mla_uq=w_mla_uq, mla_kv_norm=mla_kv_norm,
             w_mla_ukv=w_mla_ukv, w_branch=w_branch, w_o=w_o, mix_post_gain=mix_post_gain)
    depth = w_in.shape[0]
    layers = [_pack_layer(P, l) for l in range(depth)]
    y_prompt, p_rows = _trunk(x_prompt, None, layers, rel_bias)
    caches = (cache_mla_ckv, cache_mla_krope, cache_sb_k, cache_sb_v,
              cache_dsa_k, cache_dsa_v, cache_dsa_kidx)
    past = [tuple(c[l] for c in caches) for l in range(depth)]
    y_sample, s_rows = _trunk(x_sample, past, layers, rel_bias)
    return (y_prompt, y_sample) + p_rows + s_rows
```

```python
import functools
import math

import numpy as np
import jax
import jax.numpy as jnp
from jax import lax
from jax.experimental import pallas as pl
from jax.experimental.pallas import tpu as pltpu

D_MODEL = 1024
CHUNK = 64
CHUNK_SHIFT = 6
EPS = 1e-6
H_A, A_NOPE, A_ROPE, A_V = 8, 64, 32, 64
Q_LORA, KV_LORA = 384, 256
ROPE_THETA = 10000.0
H_B, D_B = 8, 64
H_C, D_C = 8, 64
H_IDX, D_IDX = 8, 64
TOPK_MAX = 256
NUM_BUCKETS, MAX_DISTANCE = 32, 256
MIX_W = 512
D_FF = 2816
IN_SPLITS = (Q_LORA, KV_LORA, A_ROPE, H_B * D_B, H_B * D_B, H_B * D_B,
             H_C * D_C, D_C, D_C, H_IDX * D_IDX, D_IDX, H_IDX,
             D_MODEL, D_MODEL, D_MODEL)
IN_OFFSETS = tuple(int(o) for o in np.cumsum((0,) + IN_SPLITS))

LANE = 128
KC = 256
TQ = 128
KB = 128
PAIR = 2 * TQ
FF_CHUNK = 256
NEG = -1e30
INT_MIN = -2 ** 31
VMEM_LIMIT = 56 * 1024 * 1024
BF16 = jnp.bfloat16
F32 = jnp.float32


def _cparams(*sem):
    return pltpu.CompilerParams(dimension_semantics=sem, vmem_limit_bytes=VMEM_LIMIT)


def _const_spec(shape):
    nd = len(shape)
    return pl.BlockSpec(shape, lambda *_: (0,) * nd, pipeline_mode=pl.Buffered(1))


def _rms(x, g):
    return x * lax.rsqrt(jnp.mean(x * x, axis=-1, keepdims=True) + EPS) * g


def _dot(a, b):
    return jnp.dot(a, b, preferred_element_type=F32)


def _dot_nt(a, b):
    return lax.dot_general(a, b, (((1,), (1,)), ((), ())), preferred_element_type=F32)


def _ffn_kernel(x_ref, pre_ref, wup_ref, wdn_ref, post_ref, o_ref):
    x = x_ref[...]
    h = _rms(x, pre_ref[...]).astype(BF16)
    acc = jnp.zeros(x.shape, F32)
    for c in range(D_FF // FF_CHUNK):
        lo = c * FF_CHUNK
        gate = _dot(h, wup_ref[:, lo:lo + FF_CHUNK])
        up = _dot(h, wup_ref[:, D_FF + lo:D_FF + lo + FF_CHUNK])
        act = (gate * jax.nn.sigmoid(gate) * up).astype(BF16)
        acc = acc + _dot(act, wdn_ref[lo:lo + FF_CHUNK, :])
    o_ref[...] = x + 0.5 * _rms(acc, post_ref[...])


def _ffn_half(x, pre_g, w_up, w_down, post_g, tm):
    n = x.shape[0]
    row = pl.BlockSpec((tm, D_MODEL), lambda i: (i, 0))
    return pl.pallas_call(
        _ffn_kernel,
        grid=(n // tm,),
        in_specs=[row, _const_spec((1, D_MODEL)), _const_spec((D_MODEL, 2 * D_FF)),
                  _const_spec((D_FF, D_MODEL)), _const_spec((1, D_MODEL))],
        out_specs=row,
        out_shape=jax.ShapeDtypeStruct((n, D_MODEL), F32),
        compiler_params=_cparams("parallel"),
        name="ffn_half",
    )(x, pre_g, w_up, w_down, post_g)


_P_QLAT = 0
_P_KVLAT = _P_QLAT + Q_LORA
_P_KR = _P_KVLAT + KV_LORA
_P_KRSW = _P_KR + LANE
_P_QB = _P_KRSW + LANE
_P_KB = _P_QB + MIX_W
_P_VB = _P_KB + MIX_W
_P_QC = _P_VB + MIX_W
_P_QI = _P_QC + MIX_W
_P_KC = _P_QI + MIX_W
_P_VC = _P_KC + LANE
_P_KI = _P_VC + LANE
_P_END = _P_KI + LANE


WI_ROWS = 16


def _in_kernel(x_ref, g_ref, w_ref, wwi_ref, qn_ref, wuq_ref, kvn_ref, cq_ref, sq_ref, ck_ref, sk_ref,
               ckv_o, krope_o, kb_o, vb_o, kc_o, vc_o, ki_o, wit_o,
               qan_o, qar_o, qb_o, qc_o, qi_o, kr4_o, kb16_o, vb16_o, kc2_o, vc2_o, ki2_o):
    u = _rms(x_ref[...], g_ref[...]).astype(BF16)
    p = _dot(u, w_ref[...])
    c_q = _rms(p[:, _P_QLAT:_P_QLAT + Q_LORA], qn_ref[...]).astype(BF16)
    qa = _dot(c_q, wuq_ref[...])
    n_nope, n_rope = H_A * A_NOPE, H_A * A_ROPE
    scale_a = (A_NOPE + A_ROPE) ** -0.5
    qan_o[...] = (qa[:, :n_nope] * scale_a).astype(BF16)
    q_rot = qa[:, n_nope:n_nope + n_rope] * cq_ref[...] + qa[:, n_nope + n_rope:] * sq_ref[...]
    qar_o[...] = (q_rot * scale_a).astype(BF16)
    ckv_o[...] = _rms(p[:, _P_KVLAT:_P_KVLAT + KV_LORA], kvn_ref[...])
    k_rot = p[:, _P_KR:_P_KR + LANE] * ck_ref[...] + p[:, _P_KRSW:_P_KRSW + LANE] * sk_ref[...]
    krope_o[...] = k_rot[:, :A_ROPE]
    kr4_o[...] = k_rot.astype(BF16)
    qb_o[...] = (p[:, _P_QB:_P_QB + MIX_W] * D_B ** -0.5).astype(BF16)
    k_b, v_b = p[:, _P_KB:_P_KB + MIX_W], p[:, _P_VB:_P_VB + MIX_W]
    kb_o[...] = k_b
    vb_o[...] = v_b
    kb16_o[...] = k_b.astype(BF16)
    vb16_o[...] = v_b.astype(BF16)
    qc_o[...] = (p[:, _P_QC:_P_QC + MIX_W] * D_C ** -0.5).astype(BF16)
    qi_o[...] = (p[:, _P_QI:_P_QI + MIX_W] * D_IDX ** -0.5).astype(BF16)
    for at, f32_o, bf16_o in ((_P_KC, kc_o, kc2_o), (_P_VC, vc_o, vc2_o), (_P_KI, ki_o, ki2_o)):
        blk = p[:, at:at + LANE]
        f32_o[...] = blk[:, :LANE // 2]
        bf16_o[...] = blk.astype(BF16)
    wit_o[...] = _dot_nt(wwi_ref[...], u) * H_IDX ** -0.5


def _mixer_in(x, g, w_pack, wwi_t, qn, wuq_pack, kvn, cq, sq, ck, sk, tm):
    n = x.shape[0]
    n_tab = cq.shape[0] // tm

    def row(w):
        return pl.BlockSpec((tm, w), lambda i: (i, 0))

    def tab(w):
        return pl.BlockSpec((tm, w), lambda i: (i % n_tab, 0))

    widths_f32 = (KV_LORA, A_ROPE, MIX_W, MIX_W, D_C, D_C, D_IDX)
    widths_bf16 = (H_A * A_NOPE, H_A * A_ROPE, MIX_W, MIX_W, MIX_W, LANE, MIX_W, MIX_W, LANE, LANE, LANE)
    return pl.pallas_call(
        _in_kernel,
        grid=(n // tm,),
        in_specs=[row(D_MODEL), _const_spec((1, D_MODEL)), _const_spec(w_pack.shape), _const_spec(wwi_t.shape),
                  _const_spec((1, Q_LORA)), _const_spec(wuq_pack.shape), _const_spec((1, KV_LORA)),
                  tab(H_A * A_ROPE), tab(H_A * A_ROPE), tab(LANE), tab(LANE)],
        out_specs=([row(w) for w in widths_f32] + [pl.BlockSpec((WI_ROWS, tm), lambda i: (0, i))]
                   + [row(w) for w in widths_bf16]),
        out_shape=([jax.ShapeDtypeStruct((n, w), F32) for w in widths_f32]
                   + [jax.ShapeDtypeStruct((WI_ROWS, n), F32)]
                   + [jax.ShapeDtypeStruct((n, w), BF16) for w in widths_bf16]),
        compiler_params=_cparams("parallel"),
        name="mixer_in",
    )(x, g, w_pack, wwi_t, qn, wuq_pack, kvn, cq, sq, ck, sk)


def _ukv_kernel(c_ref, w_ref, k_o, v_o):
    kv = _dot(c_ref[...].astype(BF16), w_ref[...])
    k_o[...] = kv[:, :H_A * A_NOPE].astype(BF16)
    v_o[...] = kv[:, H_A * A_NOPE:].astype(BF16)


def _ukv(ckv, w_pack, tm):
    n = ckv.shape[0]
    return pl.pallas_call(
        _ukv_kernel,
        grid=(n // tm,),
        in_specs=[pl.BlockSpec((tm, KV_LORA), lambda i: (i, 0)), _const_spec(w_pack.shape)],
        out_specs=[pl.BlockSpec((tm, H_A * A_NOPE), lambda i: (i, 0)),
                   pl.BlockSpec((tm, H_A * A_V), lambda i: (i, 0))],
        out_shape=[jax.ShapeDtypeStruct((n, H_A * A_NOPE), BF16),
                   jax.ShapeDtypeStruct((n, H_A * A_V), BF16)],
        compiler_params=_cparams("parallel"),
        name="mla_ukv",
    )(ckv, w_pack)


def _num_key_steps(q0, kv_len):
    k_end = jnp.minimum(kv_len, ((q0 + TQ - 1) // CHUNK + 1) * CHUNK)
    return (k_end + KC - 1) // KC


def _num_unmasked_key_steps(q0, kv_len):
    return jnp.minimum(kv_len, (q0 // CHUNK + 1) * CHUNK) // KC


def _positions(q0, k0):
    kpos = k0 + lax.broadcasted_iota(jnp.int32, (KC, TQ), 0)
    qpos = q0 + lax.broadcasted_iota(jnp.int32, (KC, TQ), 1)
    return kpos, qpos


def _chunk_visible(kpos, qpos, kv_len, tk):
    vis = (kpos >> CHUNK_SHIFT) <= (qpos >> CHUNK_SHIFT)
    return vis & (kpos < kv_len) if kv_len < tk else vis


def _head_lanes(x, width, idx):
    lane = lax.broadcasted_iota(jnp.int32, x.shape, 1) & (LANE - 1)
    return jnp.where((lane >> (width.bit_length() - 1)) == idx, x.astype(F32), 0.0).astype(x.dtype)


def _fill_transposed(src_ref, dst_ref):
    tk, width = src_ref.shape[1], src_ref.shape[2]

    def body(c, carry):
        k0 = pl.multiple_of(c * KC, KC)
        blk = src_ref[0, pl.ds(k0, KC), :].astype(F32)
        for g in range(width // LANE):
            dst_ref[g * LANE:(g + 1) * LANE, pl.ds(k0, KC)] = blk[:, g * LANE:(g + 1) * LANE].T.astype(dst_ref.dtype)
        return carry

    lax.fori_loop(0, tk // KC, body, 0)


def _store_heads_token_major(o_ref, heads):
    for g in range(len(heads) // 2):
        pair = jnp.concatenate([heads[2 * g], heads[2 * g + 1]], axis=0)
        o_ref[0, :, g * LANE:(g + 1) * LANE] = pair.T.astype(o_ref.dtype)


def _attn_call(kernel_fn, name, q_arrays, k_arrays, scratch, extra_in=(), extra_specs=()):
    b, t = q_arrays[0].shape[:2]
    in_specs = ([pl.BlockSpec((1, TQ, a.shape[2]), lambda i, j: (i, j, 0)) for a in q_arrays]
                + [pl.BlockSpec((1,) + a.shape[1:], lambda i, j: (i, 0, 0)) for a in k_arrays]
                + list(extra_specs))
    return pl.pallas_call(
        kernel_fn,
        grid=(b, t // TQ),
        in_specs=in_specs,
        out_specs=pl.BlockSpec((1, TQ, MIX_W), lambda i, j: (i, j, 0)),
        out_shape=jax.ShapeDtypeStruct((b, t, MIX_W), BF16),
        scratch_shapes=scratch,
        compiler_params=_cparams("parallel", "arbitrary"),
        name=name,
    )(*q_arrays, *k_arrays, *extra_in)


def _mla_kernel(qn_ref, qr_ref, kn_ref, kr_ref, v_ref, o_ref, qcat_ref, vt_ref, m_ref, l_ref, acc_ref,
                *, q_off, kv_len):
    tk = kn_ref.shape[1]
    qb = pl.program_id(1)
    q0 = q_off + qb * TQ
    nk = _num_key_steps(q0, kv_len)

    @pl.when(qb == 0)
    def _():
        _fill_transposed(v_ref, vt_ref)

    for h in range(H_A):
        rows = slice(h * TQ, (h + 1) * TQ)
        qcat_ref[rows, :LANE] = _head_lanes(qn_ref[0, :, (h // 2) * LANE:(h // 2 + 1) * LANE], A_NOPE, h % 2)
        qcat_ref[rows, LANE:] = _head_lanes(qr_ref[0, :, (h // 4) * LANE:(h // 4 + 1) * LANE], A_ROPE, h % 4)
    m_ref[...] = jnp.full(m_ref.shape, NEG, F32)
    l_ref[...] = jnp.zeros(l_ref.shape, F32)
    acc_ref[...] = jnp.zeros(acc_ref.shape, F32)
    pairs = range(H_A // 2)
    cols = [slice(g * PAIR, (g + 1) * PAIR) for g in pairs]

    def step(c, masked):
        k0 = pl.multiple_of(c * KC, KC)
        kr = kr_ref[0, pl.ds(k0, KC), :]
        s = [_dot_nt(jnp.concatenate([kn_ref[0, pl.ds(k0, KC), g * LANE:(g + 1) * LANE], kr], axis=1),
                     qcat_ref[cols[g], :]) for g in pairs]
        if masked:
            off = jnp.where(_chunk_visible(*_positions(q0, k0), kv_len, tk), 0.0, NEG)
            off = jnp.concatenate([off, off], axis=1)
            s = [s[g] + off for g in pairs]
        m_prev = [m_ref[:, cols[g]] for g in pairs]
        m_new = [jnp.maximum(m_prev[g], jnp.max(s[g], axis=0, keepdims=True)) for g in pairs]
        p = [jnp.exp(s[g] - m_new[g]) for g in pairs]
        alpha = [jnp.exp(m_prev[g] - m_new[g]) for g in pairs]
        p16 = [p[g].astype(BF16) for g in pairs]
        pv = [_dot(vt_ref[h * A_V:(h + 1) * A_V, pl.ds(k0, KC)], p16[h // 2][:, (h % 2) * TQ:(h % 2 + 1) * TQ])
              for h in range(H_A)]
        for g in pairs:
            l_ref[:, cols[g]] = alpha[g] * l_ref[:, cols[g]] + jnp.sum(p[g], axis=0, keepdims=True)
            acc_ref[:, cols[g]] = (alpha[g] * acc_ref[:, cols[g]]
                                   + jnp.concatenate([pv[2 * g], pv[2 * g + 1]], axis=1))
            m_ref[:, cols[g]] = m_new[g]
        return 0

    n_plain = _num_unmasked_key_steps(q0, kv_len)
    lax.fori_loop(0, n_plain, lambda c, _: step(c, False), 0)
    lax.fori_loop(n_plain, nk, lambda c, _: step(c, True), 0)
    o = acc_ref[...] / l_ref[...]
    _store_heads_token_major(o_ref, [o[:, h * TQ:(h + 1) * TQ] for h in range(H_A)])


def _mla_attention(qn, qr, kn, kr4, v, q_off, kv_len):
    tk = kn.shape[1]
    hq = H_A * TQ
    return _attn_call(
        functools.partial(_mla_kernel, q_off=q_off, kv_len=kv_len), "mla_attention",
        (qn, qr), (kn, kr4, v),
        [pltpu.VMEM((hq, 2 * LANE), BF16), pltpu.VMEM((H_A * A_V, tk), BF16),
         pltpu.VMEM((1, hq), F32), pltpu.VMEM((1, hq), F32), pltpu.VMEM((A_V, hq), F32)])


def _sb_kernel(q_ref, k_ref, v_ref, o_ref, qm_ref, vt_ref, run_ref, acc_ref, *, q_off, kv_len):
    qb = pl.program_id(1)
    q0 = q_off + qb * TQ
    nk = _num_key_steps(q0, kv_len)

    @pl.when(qb == 0)
    def _():
        _fill_transposed(v_ref, vt_ref)

    for h in range(H_B):
        qm_ref[h * TQ:(h + 1) * TQ, :] = _head_lanes(q_ref[0, :, (h // 2) * LANE:(h // 2 + 1) * LANE], D_B, h % 2)
    run_ref[...] = jnp.zeros(run_ref.shape, F32)
    acc_ref[...] = jnp.zeros(acc_ref.shape, F32)
    later = jnp.where(lax.broadcasted_iota(jnp.int32, (KC, KC), 1)
                      > lax.broadcasted_iota(jnp.int32, (KC, KC), 0), 1.0, 0.0).astype(BF16)
    pairs = range(H_B // 2)
    cols = [slice(g * PAIR, (g + 1) * PAIR) for g in pairs]

    def step(c, masked):
        k0 = pl.multiple_of(c * KC, KC)
        z = [_dot_nt(k_ref[0, pl.ds(k0, KC), g * LANE:(g + 1) * LANE], qm_ref[cols[g], :]) for g in pairs]
        soft = [jnp.log(1.0 + jnp.exp(-jnp.abs(z[g]))) for g in pairs]
        sp = [jnp.maximum(z[g], 0.0) + soft[g] for g in pairs]
        if masked:
            kpos, qpos = _positions(q0, k0)
            keep = jnp.where(kpos < qpos, 1.0, 0.0)
            keep = jnp.concatenate([keep, keep], axis=1)
            sp = [sp[g] * keep for g in pairs]
        hi = [sp[g].astype(BF16) for g in pairs]
        lo = [(sp[g] - hi[g].astype(F32)).astype(BF16) for g in pairs]
        stick = [_dot(later, hi[g]) + _dot(later, lo[g]) + run_ref[:, cols[g]] for g in pairs]
        a = [jnp.exp(jnp.minimum(z[g], 0.0) - soft[g] - stick[g]) for g in pairs]
        if masked:
            a = [a[g] * keep for g in pairs]
        a16 = [a[g].astype(BF16) for g in pairs]
        av = [_dot(vt_ref[h * D_B:(h + 1) * D_B, pl.ds(k0, KC)], a16[h // 2][:, (h % 2) * TQ:(h % 2 + 1) * TQ])
              for h in range(H_B)]
        for g in pairs:
            acc_ref[:, cols[g]] = acc_ref[:, cols[g]] + jnp.concatenate([av[2 * g], av[2 * g + 1]], axis=1)
            run_ref[:, cols[g]] = run_ref[:, cols[g]] + jnp.sum(sp[g], axis=0, keepdims=True)
        return 0

    n_plain = jnp.minimum(q0, kv_len) // KC
    lax.fori_loop(0, nk - n_plain, lambda i, _: step(nk - 1 - i, True), 0)
    lax.fori_loop(0, n_plain, lambda i, _: step(n_plain - 1 - i, False), 0)
    o = acc_ref[...]
    _store_heads_token_major(o_ref, [o[:, h * TQ:(h + 1) * TQ] for h in range(H_B)])


def _sb_attention(q, k, v, q_off, kv_len):
    tk = k.shape[1]
    hq = H_B * TQ
    return _attn_call(
        functools.partial(_sb_kernel, q_off=q_off, kv_len=kv_len), "sb_attention",
        (q,), (k, v),
        [pltpu.VMEM((hq, LANE), BF16), pltpu.VMEM((H_B * D_B, tk), BF16),
         pltpu.VMEM((1, hq), F32), pltpu.VMEM((D_B, hq), F32)])


def _t5_bucket_np(rel):
    nb = NUM_BUCKETS // 2
    max_exact = nb // 2
    ret = np.where(rel > 0, nb, 0)
    n = np.abs(rel)
    nf = np.maximum(n, 1).astype(np.float32)
    large = max_exact + (np.log(nf / max_exact) / math.log(MAX_DISTANCE / max_exact)
                         * (nb - max_exact)).astype(np.int32)
    large = np.minimum(large, nb - 1)
    return (ret + np.where(n < max_exact, n, large)).astype(np.int32)


def _bias_kernel(rb_ref, bkt_ref, o_ref):
    bkt = bkt_ref[0]
    for h in range(H_C):
        acc = jnp.zeros(bkt.shape, F32)
        for b in range(NUM_BUCKETS):
            acc = jnp.where(bkt == b, rb_ref[b, h], acc)
        o_ref[0, :, h * TQ:(h + 1) * TQ] = acc


def _bias_tiles(rel_bias, n_off):
    j = np.arange(KB)[None, :, None]
    i = np.arange(TQ)[None, None, :]
    d = np.arange(n_off)[:, None, None]
    buckets = jnp.asarray(_t5_bucket_np(j - i - d * KB))
    return pl.pallas_call(
        _bias_kernel,
        grid=(n_off,),
        in_specs=[pl.BlockSpec(memory_space=pltpu.SMEM),
                  pl.BlockSpec((1, KB, TQ), lambda d: (d, 0, 0))],
        out_specs=pl.BlockSpec((1, KB, H_C * TQ), lambda d: (d, 0, 0)),
        out_shape=jax.ShapeDtypeStruct((n_off, KB, H_C * TQ), F32),
        compiler_params=_cparams("parallel"),
        name="dsa_bias_tiles",
    )(rel_bias, buckets)


def _dsa_kernel(q_ref, qi_ref, k_ref, ki_ref, v_ref, wi_ref, bias_ref, o_ref,
                qm_ref, qim_ref, vt_ref, key_ref, m_ref, l_ref, acc_ref, *, q_off, kv_len):
    tk = k_ref.shape[1]
    qb = pl.program_id(1)
    q0 = q_off + qb * TQ
    nk = _num_key_steps(q0, kv_len)
    topk = min(TOPK_MAX, kv_len // 4)

    @pl.when(qb == 0)
    def _():
        _fill_transposed(v_ref, vt_ref)

    for h in range(H_C):
        blk = slice((h // 2) * LANE, (h // 2 + 1) * LANE)
        qm_ref[h * TQ:(h + 1) * TQ, :] = _head_lanes(q_ref[0, :, blk], D_C, h % 2)
        qim_ref[h * TQ:(h + 1) * TQ, :] = _head_lanes(qi_ref[0, :, blk], D_IDX, h % 2)

    wi = wi_ref[...]

    def score_step(c, masked):
        k0 = pl.multiple_of(c * KC, KC)
        ki = ki_ref[0, pl.ds(k0, KC), :]
        r = [_dot_nt(ki, qim_ref[g * PAIR:(g + 1) * PAIR, :]) for g in range(H_IDX // 2)]
        score = jnp.zeros((KC, TQ), F32)
        for h in range(H_IDX):
            score = score + jnp.maximum(r[h // 2][:, (h % 2) * TQ:(h % 2 + 1) * TQ], 0.0) * wi[h:h + 1, :]
        bits = lax.bitcast_convert_type(score + 0.0, jnp.int32)
        key = bits ^ ((bits >> 31) & 0x7FFFFFFF)
        if masked:
            key = jnp.where(_chunk_visible(*_positions(q0, k0), kv_len, tk), key, INT_MIN)
        key_ref[pl.ds(k0, KC), :] = key
        return 0

    n_plain = _num_unmasked_key_steps(q0, kv_len)
    lax.fori_loop(0, n_plain, lambda c, _: score_step(c, False), 0)
    lax.fori_loop(n_plain, nk, lambda c, _: score_step(c, True), 0)

    def count(pred):
        def body(c, acc):
            k0 = pl.multiple_of(c * KC, KC)
            hit = jnp.where(pred(key_ref[pl.ds(k0, KC), :]), 1, 0)
            return acc + jnp.sum(hit.reshape(KC // 8, 8, TQ), axis=0)
        acc = lax.fori_loop(0, nk, body, jnp.zeros((8, TQ), jnp.int32))
        return jnp.sum(acc, axis=0, keepdims=True)

    def search(i, thr):
        cand = thr + jnp.left_shift(jnp.int32(1), 31 - i)
        cnt = count(lambda key: key >= cand)
        return jnp.where(cnt >= topk, cand, thr)

    thr = lax.fori_loop(0, 32, search, jnp.full((1, TQ), INT_MIN, jnp.int32))

    n_ge = count(lambda key: key >= thr)
    surplus = jnp.where(thr > INT_MIN, n_ge - topk, 0)

    @pl.when(jnp.max(surplus) > 0)
    def _():
        need = topk - count(lambda key: key > thr)
        upto = jnp.where(lax.broadcasted_iota(jnp.int32, (KC, KC), 1)
                         <= lax.broadcasted_iota(jnp.int32, (KC, KC), 0), 1.0, 0.0).astype(BF16)

        def body(c, seen):
            k0 = pl.multiple_of(c * KC, KC)
            key = key_ref[pl.ds(k0, KC), :]
            tie = key == thr
            rank = seen + _dot(upto, jnp.where(tie, 1.0, 0.0).astype(BF16)).astype(jnp.int32)
            key_ref[pl.ds(k0, KC), :] = jnp.where(tie & (rank > need), INT_MIN, key)
            return seen + jnp.sum(jnp.where(tie, 1, 0), axis=0, keepdims=True)

        lax.fori_loop(0, nk, body, jnp.zeros((1, TQ), jnp.int32))

    thr_sel = jnp.maximum(thr, INT_MIN + 1)
    m_ref[...] = jnp.full(m_ref.shape, NEG, F32)
    l_ref[...] = jnp.zeros(l_ref.shape, F32)
    acc_ref[...] = jnp.zeros(acc_ref.shape, F32)
    d0 = q0 // KB

    def attend(c, carry):
        k0 = pl.multiple_of(c * KC, KC)
        kc = k_ref[0, pl.ds(k0, KC), :]
        vt = vt_ref[:D_C, pl.ds(k0, KC)]
        off = jnp.where(key_ref[pl.ds(k0, KC), :] >= thr_sel, 0.0, NEG)
        off = jnp.concatenate([off, off], axis=1)
        tiles = [jnp.maximum(d0 - (KC // KB) * c - r, 0) for r in range(KC // KB)]
        groups = range(H_C // 2)
        cols = [slice(g * PAIR, (g + 1) * PAIR) for g in groups]
        s = [_dot_nt(kc, qm_ref[cols[g], :])
             + jnp.concatenate([bias_ref[t, :, cols[g]] for t in tiles], axis=0) + off for g in groups]
        m_prev = [m_ref[:, cols[g]] for g in groups]
        m_new = [jnp.maximum(m_prev[g], jnp.max(s[g], axis=0, keepdims=True)) for g in groups]
        p = [jnp.exp(s[g] - m_new[g]) for g in groups]
        alpha = [jnp.exp(m_prev[g] - m_new[g]) for g in groups]
        pv = [_dot(vt, p[g].astype(BF16)) for g in groups]
        for g in groups:
            l_ref[:, cols[g]] = alpha[g] * l_ref[:, cols[g]] + jnp.sum(p[g], axis=0, keepdims=True)
            acc_ref[:, cols[g]] = alpha[g] * acc_ref[:, cols[g]] + pv[g]
            m_ref[:, cols[g]] = m_new[g]
        return carry

    lax.fori_loop(0, nk, attend, 0)
    o = acc_ref[...] / l_ref[...]
    _store_heads_token_major(o_ref, [o[:, h * TQ:(h + 1) * TQ] for h in range(H_C)])


def _dsa_attention(q, qi, k2, ki2, v2, wi_t, bias, q_off, kv_len):
    tk = k2.shape[1]
    nq = q.shape[1] // TQ
    hq = H_C * TQ
    return _attn_call(
        functools.partial(_dsa_kernel, q_off=q_off, kv_len=kv_len), "dsa_attention",
        (q, qi), (k2, ki2, v2),
        [pltpu.VMEM((hq, LANE), BF16), pltpu.VMEM((hq, LANE), BF16), pltpu.VMEM((LANE, tk), BF16),
         pltpu.VMEM((tk, TQ), jnp.int32), pltpu.VMEM((1, hq), F32), pltpu.VMEM((1, hq), F32),
         pltpu.VMEM((D_C, hq), F32)],
        extra_in=(wi_t, bias),
        extra_specs=(pl.BlockSpec((WI_ROWS, TQ), lambda i, j: (0, i * nq + j)), _const_spec(bias.shape)))


def _merge_kernel(x_ref, oa_ref, ob_ref, oc_ref, g_ref, wg_ref, wbr_ref, wo_ref, post_ref, o_ref):
    x = x_ref[...]
    u = _rms(x, g_ref[...]).astype(BF16)
    merged = jnp.zeros(x.shape, F32)
    for i, o_br in enumerate((oa_ref, ob_ref, oc_ref)):
        gate = jax.nn.sigmoid(_dot(u, wg_ref[:, i * D_MODEL:(i + 1) * D_MODEL]))
        merged = merged + gate * _dot(o_br[...], wbr_ref[i])
    y = _dot(merged.astype(BF16), wo_ref[...])
    o_ref[...] = x + _rms(y, post_ref[...])


def _merge(x, oa, ob, oc, g, w_gate, w_br, w_o, post_g, tm):
    n = x.shape[0]
    row = pl.BlockSpec((tm, D_MODEL), lambda i: (i, 0))
    br = pl.BlockSpec((tm, MIX_W), lambda i: (i, 0))
    return pl.pallas_call(
        _merge_kernel,
        grid=(n // tm,),
        in_specs=[row, br, br, br, _const_spec((1, D_MODEL)), _const_spec(w_gate.shape),
                  _const_spec(w_br.shape), _const_spec(w_o.shape), _const_spec((1, D_MODEL))],
        out_specs=row,
        out_shape=jax.ShapeDtypeStruct((n, D_MODEL), F32),
        compiler_params=_cparams("parallel"),
        name="mixer_merge",
    )(x, oa, ob, oc, g, w_gate, w_br, w_o, post_g)


def _swap_halves(w, group):
    k, n = w.shape
    w = w.reshape(k, n // group, 2, group // 2)
    return w[:, :, ::-1, :].reshape(k, n)


def _pack_layer(P, l):
    w_in = P['w_in'][l]
    (q_lat, kv_lat, k_r, q_b, k_b, v_b, q_c, k_c, v_c, q_i, k_i, w_i, g_a, g_b, g_c) = [
        w_in[:, IN_OFFSETS[i]:IN_OFFSETS[i + 1]] for i in range(len(IN_SPLITS))]
    w_pack = jnp.concatenate(
        [q_lat, kv_lat, jnp.tile(k_r, (1, LANE // A_ROPE)), jnp.tile(_swap_halves(k_r, A_ROPE), (1, LANE // A_ROPE)),
         q_b, k_b, v_b, q_c, q_i, k_c, k_c, v_c, v_c, k_i, k_i], axis=1).astype(BF16)
    assert w_pack.shape[1] == _P_END
    wwi_t = jnp.pad(w_i.T, ((0, WI_ROWS - H_IDX), (0, 0))).astype(BF16)
    uq = P['w_mla_uq'][l].reshape(Q_LORA, H_A, A_NOPE + A_ROPE)
    uq_nope = uq[:, :, :A_NOPE].reshape(Q_LORA, H_A * A_NOPE)
    uq_rope = uq[:, :, A_NOPE:].reshape(Q_LORA, H_A * A_ROPE)
    wuq_pack = jnp.concatenate([uq_nope, uq_rope, _swap_halves(uq_rope, A_ROPE)], axis=1).astype(BF16)
    ukv = P['w_mla_ukv'][l].reshape(KV_LORA, H_A, A_NOPE + A_V)
    wukv_pack = jnp.concatenate([ukv[:, :, :A_NOPE].reshape(KV_LORA, H_A * A_NOPE),
                                 ukv[:, :, A_NOPE:].reshape(KV_LORA, H_A * A_V)], axis=1).astype(BF16)
    return dict(
        w_pack=w_pack, wwi_t=wwi_t, wuq_pack=wuq_pack, wukv_pack=wukv_pack,
        w_gate=jnp.concatenate([g_a, g_b, g_c], axis=1).astype(BF16),
        w_br=P['w_branch'][l].astype(BF16), w_o=P['w_o'][l].astype(BF16),
        mix_pre=P['mix_pre_gain'][l][None], mix_post=P['mix_post_gain'][l][None],
        q_norm=P['mla_q_norm'][l][None], kv_norm=P['mla_kv_norm'][l][None],
        ffn=[dict(pre=P['ffn_pre_gain'][l, i][None], post=P['ffn_post_gain'][l, i][None],
                  w_up=P['w_ffn_up'][l, i].astype(BF16), w_down=P['w_ffn_down'][l, i].astype(BF16))
             for i in range(2)])


def _rope_tables(pos, rows):
    half = A_ROPE // 2
    inv = ROPE_THETA ** (-jnp.arange(half, dtype=F32) / half)
    ang = pos.astype(F32)[:, None] * inv[None, :]
    c, s = jnp.cos(ang), jnp.sin(ang)
    ck = jnp.concatenate([c, c], axis=1)
    sk = jnp.concatenate([-s, s], axis=1)
    rep = max(1, rows // pos.shape[0])
    ck, sk = jnp.tile(ck, (rep, 1)), jnp.tile(sk, (rep, 1))
    return (jnp.tile(ck, (1, H_A)), jnp.tile(sk, (1, H_A)),
            jnp.tile(ck, (1, LANE // A_ROPE)), jnp.tile(sk, (1, LANE // A_ROPE)))


def _pad_axis1(a, size):
    if a.shape[1] == size:
        return a
    pad = [(0, 0)] * a.ndim
    pad[1] = (0, size - a.shape[1])
    return jnp.pad(a, pad)


def _mixer(x, L, past, bias_tiles, b, t, tm):
    n = b * t
    p_len = 0 if past is None else past[0].shape[1]
    kv_len = p_len + t
    tk = -(-kv_len // KC) * KC
    tqp = -(-t // TQ) * TQ
    pos = p_len + jnp.arange(t, dtype=jnp.int32)
    cq, sq, ck, sk = _rope_tables(pos, tm)
    (ckv, krope, kb, vb, kc, vc, ki, wi_t, qan, qar, qb, qc, qi, kr4, kb16, vb16, kc2, vc2, ki2) = _mixer_in(
        x, L['mix_pre'], L['w_pack'], L['wwi_t'], L['q_norm'], L['wuq_pack'], L['kv_norm'], cq, sq, ck, sk, tm)
    new_rows = (ckv.reshape(b, t, KV_LORA), krope.reshape(b, t, A_ROPE),
                kb.reshape(b, t, H_B, D_B), vb.reshape(b, t, H_B, D_B),
                kc.reshape(b, t, D_C), vc.reshape(b, t, D_C), ki.reshape(b, t, D_IDX))

    def keys(a):
        return _pad_axis1(a, tk)

    if past is None:
        ckv_all = keys(new_rows[0])
        kr4_all, kb_all, vb_all, kc_all, vc_all, ki_all = (
            keys(a.reshape(b, t, -1)) for a in (kr4, kb16, vb16, kc2, vc2, ki2))
    else:
        def with_past(pa, nr, w):
            return keys(jnp.concatenate([pa.reshape(b, p_len, w), nr.reshape(b, t, w)], axis=1))

        def lanes(a, rep):
            return jnp.tile(a.astype(BF16), (1, 1, rep))

        ckv_all = with_past(past[0], ckv, KV_LORA)
        kr4_all = lanes(with_past(past[1], krope, A_ROPE), LANE // A_ROPE)
        kb_all = with_past(past[2], kb, MIX_W).astype(BF16)
        vb_all = with_past(past[3], vb, MIX_W).astype(BF16)
        kc_all, vc_all, ki_all = (lanes(with_past(pa, nr, w), LANE // w)
                                  for pa, nr, w in ((past[4], kc, D_C), (past[5], vc, D_C), (past[6], ki, D_IDX)))

    def queries(a):
        return _pad_axis1(a.reshape(b, t, -1), tqp)

    def attended(o):
        return o[:, :t].reshape(n, MIX_W)

    k_nope, v_a = _ukv(ckv_all.reshape(b * tk, KV_LORA), L['wukv_pack'], 1024 if (b * tk) % 1024 == 0 else tk)
    o_a = _mla_attention(queries(qan), queries(qar), k_nope.reshape(b, tk, MIX_W), kr4_all,
                         v_a.reshape(b, tk, MIX_W), p_len, kv_len)
    o_b = _sb_attention(queries(qb), kb_all, vb_all, p_len, kv_len)
    wi_t = _pad_axis1(wi_t.reshape(WI_ROWS * b, t), tqp).reshape(WI_ROWS, b * tqp)
    o_c = _dsa_attention(queries(qc), queries(qi), kc_all, ki_all, vc_all, wi_t, bias_tiles, p_len, kv_len)
    x = _merge(x, attended(o_a), attended(o_b), attended(o_c), L['mix_pre'], L['w_gate'], L['w_br'], L['w_o'],
               L['mix_post'], tm)
    return x, new_rows


def _trunk(x, past, layers, rel_bias):
    b, t, _ = x.shape
    n = b * t
    tm = min(512, n)
    p_len = 0 if past is None else past[0][0].shape[1]
    assert p_len % KB == 0 and TQ == KB
    n_off = (p_len + (-(-t // TQ) - 1) * TQ) // KB + 1
    bias_tiles = _bias_tiles(rel_bias, n_off)
    x = x.reshape(n, D_MODEL)
    rows = []
    for l, L in enumerate(layers):
        f = L['ffn'][0]
        x = _ffn_half(x, f['pre'], f['w_up'], f['w_down'], f['post'], tm)
        x, new = _mixer(x, L, None if past is None else past[l], bias_tiles, b, t, tm)
        f = L['ffn'][1]
        x = _ffn_half(x, f['pre'], f['w_up'], f['w_down'], f['post'], tm)
        rows.append(new)
    return x.reshape(b, t, D_MODEL), tuple(jnp.stack(g) for g in zip(*rows))


def kernel(x_prompt, x_sample, cache_mla_ckv, cache_mla_krope, cache_sb_k, cache_sb_v, cache_dsa_k, cache_dsa_v, cache_dsa_kidx, rel_bias, ffn_pre_gain, w_ffn_up, w_ffn_down, ffn_post_gain, mix_pre_gain, w_in, mla_q_norm, w_mla_uq, mla_kv_norm, w_mla_ukv, w_branch, w_o, mix_post_gain):
    P = dict(ffn_pre_gain=ffn_pre_gain, w_ffn_up=w_ffn_up, w_ffn_down=w_ffn_down,
             ffn_post_gain=ffn_post_gain, mix_pre_gain=mix_pre_gain, w_in=w_in,
             mla_q_norm=mla_q_norm, w_mla_uq=w_mla_uq, mla_kv_norm=mla_kv_norm,
             w_mla_ukv=w_mla_ukv, w_branch=w_branch, w_o=w_o, mix_post_gain=mix_post_gain)
    depth = w_in.shape[0]
    layers = [_pack_layer(P, l) for l in range(depth)]
    y_prompt, p_rows = _trunk(x_prompt, None, layers, rel_bias)
    caches = (cache_mla_ckv, cache_mla_krope, cache_sb_k, cache_sb_v,
              cache_dsa_k, cache_dsa_v, cache_dsa_kidx)
    past = [tuple(c[l] for c in caches) for l in range(depth)]
    y_sample, s_rows = _trunk(x_sample, past, layers, rel_bias)
    return (y_prompt, y_sample) + p_rows + s_rows
```

```python
import functools
import math

import numpy as np
import jax
import jax.numpy as jnp
from jax import lax
from jax.experimental import pallas as pl
from jax.experimental.pallas import tpu as pltpu

D_MODEL = 1024
CHUNK = 64
CHUNK_SHIFT = 6
EPS = 1e-6
H_A, A_NOPE, A_ROPE, A_V = 8, 64, 32, 64
Q_LORA, KV_LORA = 384, 256
ROPE_THETA = 10000.0
H_B, D_B = 8, 64
H_C, D_C = 8, 64
H_IDX, D_IDX = 8, 64
TOPK_MAX = 256
NUM_BUCKETS, MAX_DISTANCE = 32, 256
MIX_W = 512
D_FF = 2816
IN_SPLITS = (Q_LORA, KV_LORA, A_ROPE, H_B * D_B, H_B * D_B, H_B * D_B,
             H_C * D_C, D_C, D_C, H_IDX * D_IDX, D_IDX, H_IDX,
             D_MODEL, D_MODEL, D_MODEL)
IN_OFFSETS = tuple(int(o) for o in np.cumsum((0,) + IN_SPLITS))

LANE = 128
KC = 256
QUERY_TILE_CAP = 256
KB = 128
FF_CHUNK = 256
NEG = -1e30
INT_MIN = -2 ** 31
I16_MIN, I16_MAX = -2 ** 15, 2 ** 15 - 1
VMEM_BYTES_V7X = 64 * 1024 * 1024
VMEM_LIMIT = VMEM_BYTES_V7X - 8 * 1024 * 1024
BF16 = jnp.bfloat16
F32 = jnp.float32


def _cparams(*sem):
    return pltpu.CompilerParams(dimension_semantics=sem, vmem_limit_bytes=VMEM_LIMIT)


def _const_spec(shape):
    nd = len(shape)
    return pl.BlockSpec(shape, lambda *_: (0,) * nd, pipeline_mode=pl.Buffered(1))


def _rms(x, g):
    return x * lax.rsqrt(jnp.mean(x * x, axis=-1, keepdims=True) + EPS) * g


def _dot(a, b):
    return jnp.dot(a, b, preferred_element_type=F32)


def _dot_nt(a, b):
    return lax.dot_general(a, b, (((1,), (1,)), ((), ())), preferred_element_type=F32)


def _ffn_kernel(x_ref, pre_ref, wup_ref, wdn_ref, post_ref, o_ref):
    x = x_ref[...]
    h = _rms(x, pre_ref[...]).astype(BF16)
    acc = jnp.zeros(x.shape, F32)
    for c in range(D_FF // FF_CHUNK):
        lo = c * FF_CHUNK
        gate = _dot(h, wup_ref[:, lo:lo + FF_CHUNK])
        up = _dot(h, wup_ref[:, D_FF + lo:D_FF + lo + FF_CHUNK])
        act = (gate * jax.nn.sigmoid(gate) * up).astype(BF16)
        acc = acc + _dot(act, wdn_ref[lo:lo + FF_CHUNK, :])
    o_ref[...] = x + 0.5 * _rms(acc, post_ref[...])


def _ffn_half(x, pre_g, w_up, w_down, post_g, tm):
    n = x.shape[0]
    row = pl.BlockSpec((tm, D_MODEL), lambda i: (i, 0))
    return pl.pallas_call(
        _ffn_kernel,
        grid=(n // tm,),
        in_specs=[row, _const_spec((1, D_MODEL)), _const_spec((D_MODEL, 2 * D_FF)),
                  _const_spec((D_FF, D_MODEL)), _const_spec((1, D_MODEL))],
        out_specs=row,
        out_shape=jax.ShapeDtypeStruct((n, D_MODEL), F32),
        compiler_params=_cparams("parallel"),
        name="ffn_half",
    )(x, pre_g, w_up, w_down, post_g)


_P_QLAT = 0
_P_KVLAT = _P_QLAT + Q_LORA
_P_KR = _P_KVLAT + KV_LORA
_P_KRSW = _P_KR + LANE
_P_QB = _P_KRSW + LANE
_P_KB = _P_QB + MIX_W
_P_VB = _P_KB + MIX_W
_P_QC = _P_VB + MIX_W
_P_QI = _P_QC + MIX_W
_P_KC = _P_QI + MIX_W
_P_VC = _P_KC + LANE
_P_KI = _P_VC + LANE
_P_END = _P_KI + LANE


WI_ROWS = 16


def _in_kernel(x_ref, g_ref, w_ref, wwi_ref, qn_ref, wuq_ref, kvn_ref, cq_ref, sq_ref, ck_ref, sk_ref,
               ckv_o, krope_o, kb_o, vb_o, kc_o, vc_o, ki_o, wit_o,
               qan_o, qar_o, qb_o, qc_o, qi_o, kr4_o, kb16_o, vb16_o, kc2_o, vc2_o, ki2_o):
    u = _rms(x_ref[...], g_ref[...]).astype(BF16)
    p = _dot(u, w_ref[...])
    c_q = _rms(p[:, _P_QLAT:_P_QLAT + Q_LORA], qn_ref[...]).astype(BF16)
    qa = _dot(c_q, wuq_ref[...])
    n_nope, n_rope = H_A * A_NOPE, H_A * A_ROPE
    scale_a = (A_NOPE + A_ROPE) ** -0.5
    qan_o[...] = (qa[:, :n_nope] * scale_a).astype(BF16)
    q_rot = qa[:, n_nope:n_nope + n_rope] * cq_ref[...] + qa[:, n_nope + n_rope:] * sq_ref[...]
    qar_o[...] = (q_rot * scale_a).astype(BF16)
    ckv_o[...] = _rms(p[:, _P_KVLAT:_P_KVLAT + KV_LORA], kvn_ref[...])
    k_rot = p[:, _P_KR:_P_KR + LANE] * ck_ref[...] + p[:, _P_KRSW:_P_KRSW + LANE] * sk_ref[...]
    krope_o[...] = k_rot[:, :A_ROPE]
    kr4_o[...] = k_rot.astype(BF16)
    qb_o[...] = (p[:, _P_QB:_P_QB + MIX_W] * D_B ** -0.5).astype(BF16)
    k_b, v_b = p[:, _P_KB:_P_KB + MIX_W], p[:, _P_VB:_P_VB + MIX_W]
    kb_o[...] = k_b
    vb_o[...] = v_b
    kb16_o[...] = k_b.astype(BF16)
    vb16_o[...] = v_b.astype(BF16)
    qc_o[...] = (p[:, _P_QC:_P_QC + MIX_W] * D_C ** -0.5).astype(BF16)
    qi_o[...] = (p[:, _P_QI:_P_QI + MIX_W] * D_IDX ** -0.5).astype(BF16)
    for at, f32_o, bf16_o in ((_P_KC, kc_o, kc2_o), (_P_VC, vc_o, vc2_o), (_P_KI, ki_o, ki2_o)):
        blk = p[:, at:at + LANE]
        f32_o[...] = blk[:, :LANE // 2]
        bf16_o[...] = blk.astype(BF16)
    wit_o[...] = _dot_nt(wwi_ref[...], u) * H_IDX ** -0.5


def _mixer_in(x, g, w_pack, wwi_t, qn, wuq_pack, kvn, cq, sq, ck, sk, tm):
    n = x.shape[0]
    n_tab = cq.shape[0] // tm

    def row(w):
        return pl.BlockSpec((tm, w), lambda i: (i, 0))

    def tab(w):
        return pl.BlockSpec((tm, w), lambda i: (i % n_tab, 0))

    widths_f32 = (KV_LORA, A_ROPE, MIX_W, MIX_W, D_C, D_C, D_IDX)
    widths_bf16 = (H_A * A_NOPE, H_A * A_ROPE, MIX_W, MIX_W, MIX_W, LANE, MIX_W, MIX_W, LANE, LANE, LANE)
    return pl.pallas_call(
        _in_kernel,
        grid=(n // tm,),
        in_specs=[row(D_MODEL), _const_spec((1, D_MODEL)), _const_spec(w_pack.shape), _const_spec(wwi_t.shape),
                  _const_spec((1, Q_LORA)), _const_spec(wuq_pack.shape), _const_spec((1, KV_LORA)),
                  tab(H_A * A_ROPE), tab(H_A * A_ROPE), tab(LANE), tab(LANE)],
        out_specs=([row(w) for w in widths_f32] + [pl.BlockSpec((WI_ROWS, tm), lambda i: (0, i))]
                   + [row(w) for w in widths_bf16]),
        out_shape=([jax.ShapeDtypeStruct((n, w), F32) for w in widths_f32]
                   + [jax.ShapeDtypeStruct((WI_ROWS, n), F32)]
                   + [jax.ShapeDtypeStruct((n, w), BF16) for w in widths_bf16]),
        compiler_params=_cparams("parallel"),
        name="mixer_in",
    )(x, g, w_pack, wwi_t, qn, wuq_pack, kvn, cq, sq, ck, sk)


def _ukv_kernel(c_ref, w_ref, k_o, v_o):
    kv = _dot(c_ref[...].astype(BF16), w_ref[...])
    k_o[...] = kv[:, :H_A * A_NOPE].astype(BF16)
    v_o[...] = kv[:, H_A * A_NOPE:].astype(BF16)


def _ukv(ckv, w_pack, tm):
    n = ckv.shape[0]
    return pl.pallas_call(
        _ukv_kernel,
        grid=(n // tm,),
        in_specs=[pl.BlockSpec((tm, KV_LORA), lambda i: (i, 0)), _const_spec(w_pack.shape)],
        out_specs=[pl.BlockSpec((tm, H_A * A_NOPE), lambda i: (i, 0)),
                   pl.BlockSpec((tm, H_A * A_V), lambda i: (i, 0))],
        out_shape=[jax.ShapeDtypeStruct((n, H_A * A_NOPE), BF16),
                   jax.ShapeDtypeStruct((n, H_A * A_V), BF16)],
        compiler_params=_cparams("parallel"),
        name="mla_ukv",
    )(ckv, w_pack)


def _num_key_steps(q0, tq, kv_len):
    k_end = jnp.minimum(kv_len, ((q0 + tq - 1) // CHUNK + 1) * CHUNK)
    return (k_end + KC - 1) // KC


def _num_unmasked_key_steps(q0, kv_len):
    return jnp.minimum(kv_len, (q0 // CHUNK + 1) * CHUNK) // KC


def _positions(q0, k0, tq):
    kpos = k0 + lax.broadcasted_iota(jnp.int32, (KC, tq), 0)
    qpos = q0 + lax.broadcasted_iota(jnp.int32, (KC, tq), 1)
    return kpos, qpos


def _chunk_visible(kpos, qpos, kv_len, tk):
    vis = (kpos >> CHUNK_SHIFT) <= (qpos >> CHUNK_SHIFT)
    return vis & (kpos < kv_len) if kv_len < tk else vis


def _head_lanes(x, width, idx):
    lane = lax.broadcasted_iota(jnp.int32, (1, x.shape[1]), 1) & (LANE - 1)
    return x * jnp.where((lane >> (width.bit_length() - 1)) == idx, 1.0, 0.0).astype(x.dtype)


def _fill_transposed(src_ref, dst_ref):
    tk, width = src_ref.shape[1], src_ref.shape[2]

    def body(c, carry):
        k0 = pl.multiple_of(c * KC, KC)
        blk = src_ref[0, pl.ds(k0, KC), :].astype(F32)
        for g in range(width // LANE):
            dst_ref[g * LANE:(g + 1) * LANE, pl.ds(k0, KC)] = blk[:, g * LANE:(g + 1) * LANE].T.astype(dst_ref.dtype)
        return carry

    lax.fori_loop(0, tk // KC, body, 0)


def _store_heads_token_major(o_ref, heads):
    for g in range(len(heads) // 2):
        pair = jnp.concatenate([heads[2 * g], heads[2 * g + 1]], axis=0)
        o_ref[0, :, g * LANE:(g + 1) * LANE] = pair.T.astype(o_ref.dtype)


def _attn_call(kernel_fn, name, tq, q_arrays, k_arrays, scratch, extra_in=(), extra_specs=()):
    b, t = q_arrays[0].shape[:2]
    in_specs = ([pl.BlockSpec((1, tq, a.shape[2]), lambda i, j: (i, j, 0)) for a in q_arrays]
                + [pl.BlockSpec((1,) + a.shape[1:], lambda i, j: (i, 0, 0)) for a in k_arrays]
                + list(extra_specs))
    return pl.pallas_call(
        kernel_fn,
        grid=(b, t // tq),
        in_specs=in_specs,
        out_specs=pl.BlockSpec((1, tq, MIX_W), lambda i, j: (i, j, 0)),
        out_shape=jax.ShapeDtypeStruct((b, t, MIX_W), BF16),
        scratch_shapes=scratch,
        compiler_params=_cparams("parallel", "arbitrary"),
        name=name,
    )(*q_arrays, *k_arrays, *extra_in)


def _mla_kernel(qn_ref, qr_ref, kn_ref, kr_ref, v_ref, o_ref, qcat_ref, vt_ref, m_ref, l_ref, acc_ref,
                *, q_off, kv_len):
    tq, tk = qn_ref.shape[1], kn_ref.shape[1]
    pair = 2 * tq
    qb = pl.program_id(1)
    q0 = q_off + qb * tq
    nk = _num_key_steps(q0, tq, kv_len)

    @pl.when(qb == 0)
    def _():
        _fill_transposed(v_ref, vt_ref)

    for h in range(H_A):
        rows = slice(h * tq, (h + 1) * tq)
        qcat_ref[rows, :LANE] = _head_lanes(qn_ref[0, :, (h // 2) * LANE:(h // 2 + 1) * LANE], A_NOPE, h % 2)
        qcat_ref[rows, LANE:] = _head_lanes(qr_ref[0, :, (h // 4) * LANE:(h // 4 + 1) * LANE], A_ROPE, h % 4)
    m_ref[...] = jnp.full(m_ref.shape, NEG, F32)
    l_ref[...] = jnp.zeros(l_ref.shape, F32)
    acc_ref[...] = jnp.zeros(acc_ref.shape, F32)
    pairs = range(H_A // 2)
    cols = [slice(g * pair, (g + 1) * pair) for g in pairs]

    def step(c, masked):
        k0 = pl.multiple_of(c * KC, KC)
        kr = kr_ref[0, pl.ds(k0, KC), :]
        s = [_dot_nt(jnp.concatenate([kn_ref[0, pl.ds(k0, KC), g * LANE:(g + 1) * LANE], kr], axis=1),
                     qcat_ref[cols[g], :]) for g in pairs]
        if masked:
            off = jnp.where(_chunk_visible(*_positions(q0, k0, tq), kv_len, tk), 0.0, NEG)
            off = jnp.concatenate([off, off], axis=1)
            s = [s[g] + off for g in pairs]
        m_prev = [m_ref[:, cols[g]] for g in pairs]
        m_new = [jnp.maximum(m_prev[g], jnp.max(s[g], axis=0, keepdims=True)) for g in pairs]
        p = [jnp.exp(s[g] - m_new[g]) for g in pairs]
        alpha = [jnp.exp(m_prev[g] - m_new[g]) for g in pairs]
        p16 = [p[g].astype(BF16) for g in pairs]
        pv = [_dot(vt_ref[h * A_V:(h + 1) * A_V, pl.ds(k0, KC)], p16[h // 2][:, (h % 2) * tq:(h % 2 + 1) * tq])
              for h in range(H_A)]
        for g in pairs:
            l_ref[:, cols[g]] = alpha[g] * l_ref[:, cols[g]] + jnp.sum(p[g], axis=0, keepdims=True)
            acc_ref[:, cols[g]] = (alpha[g] * acc_ref[:, cols[g]]
                                   + jnp.concatenate([pv[2 * g], pv[2 * g + 1]], axis=1))
            m_ref[:, cols[g]] = m_new[g]
        return 0

    n_plain = _num_unmasked_key_steps(q0, kv_len)
    lax.fori_loop(0, n_plain, lambda c, _: step(c, False), 0)
    lax.fori_loop(n_plain, nk, lambda c, _: step(c, True), 0)
    o = acc_ref[...] / l_ref[...]
    _store_heads_token_major(o_ref, [o[:, h * tq:(h + 1) * tq] for h in range(H_A)])


def _mla_attention(qn, qr, kn, kr4, v, tq, q_off, kv_len):
    tk = kn.shape[1]
    hq = H_A * tq
    return _attn_call(
        functools.partial(_mla_kernel, q_off=q_off, kv_len=kv_len), "mla_attention", tq,
        (qn, qr), (kn, kr4, v),
        [pltpu.VMEM((hq, 2 * LANE), BF16), pltpu.VMEM((H_A * A_V, tk), BF16),
         pltpu.VMEM((1, hq), F32), pltpu.VMEM((1, hq), F32), pltpu.VMEM((A_V, hq), F32)])


def _sb_kernel(q_ref, k_ref, v_ref, o_ref, qm_ref, vt_ref, run_ref, acc_ref, *, q_off, kv_len):
    tq = q_ref.shape[1]
    pair = 2 * tq
    qb = pl.program_id(1)
    q0 = q_off + qb * tq
    nk = _num_key_steps(q0, tq, kv_len)

    @pl.when(qb == 0)
    def _():
        _fill_transposed(v_ref, vt_ref)

    for h in range(H_B):
        qm_ref[h * tq:(h + 1) * tq, :] = _head_lanes(q_ref[0, :, (h // 2) * LANE:(h // 2 + 1) * LANE], D_B, h % 2)
    run_ref[...] = jnp.zeros(run_ref.shape, F32)
    acc_ref[...] = jnp.zeros(acc_ref.shape, F32)
    from_here = jnp.where(lax.broadcasted_iota(jnp.int32, (KC, KC), 1)
                          >= lax.broadcasted_iota(jnp.int32, (KC, KC), 0), 1.0, 0.0).astype(BF16)
    pairs = range(H_B // 2)
    cols = [slice(g * pair, (g + 1) * pair) for g in pairs]

    def step(c, masked):
        k0 = pl.multiple_of(c * KC, KC)
        z = [_dot_nt(k_ref[0, pl.ds(k0, KC), g * LANE:(g + 1) * LANE], qm_ref[cols[g], :]) for g in pairs]
        soft = [jnp.log(1.0 + jnp.exp(-jnp.abs(z[g]))) for g in pairs]
        sp = [jnp.maximum(z[g], 0.0) + soft[g] for g in pairs]
        if masked:
            kpos, qpos = _positions(q0, k0, tq)
            keep = jnp.where(kpos < qpos, 1.0, 0.0)
            keep = jnp.concatenate([keep, keep], axis=1)
            sp = [sp[g] * keep for g in pairs]
        hi = [sp[g].astype(BF16) for g in pairs]
        lo = [(sp[g] - hi[g].astype(F32)).astype(BF16) for g in pairs]
        stick = [_dot(from_here, hi[g]) + _dot(from_here, lo[g]) + run_ref[:, cols[g]] for g in pairs]
        a = [jnp.exp(z[g] - stick[g]) for g in pairs]
        if masked:
            a = [a[g] * keep for g in pairs]
        a16 = [a[g].astype(BF16) for g in pairs]
        av = [_dot(vt_ref[h * D_B:(h + 1) * D_B, pl.ds(k0, KC)], a16[h // 2][:, (h % 2) * tq:(h % 2 + 1) * tq])
              for h in range(H_B)]
        for g in pairs:
            acc_ref[:, cols[g]] = acc_ref[:, cols[g]] + jnp.concatenate([av[2 * g], av[2 * g + 1]], axis=1)
            run_ref[:, cols[g]] = run_ref[:, cols[g]] + jnp.sum(sp[g], axis=0, keepdims=True)
        return 0

    n_plain = jnp.minimum(q0, kv_len) // KC
    lax.fori_loop(0, nk - n_plain, lambda i, _: step(nk - 1 - i, True), 0)
    lax.fori_loop(0, n_plain, lambda i, _: step(n_plain - 1 - i, False), 0)
    o = acc_ref[...]
    _store_heads_token_major(o_ref, [o[:, h * tq:(h + 1) * tq] for h in range(H_B)])


def _sb_attention(q, k, v, tq, q_off, kv_len):
    tk = k.shape[1]
    hq = H_B * tq
    return _attn_call(
        functools.partial(_sb_kernel, q_off=q_off, kv_len=kv_len), "sb_attention", tq,
        (q,), (k, v),
        [pltpu.VMEM((hq, LANE), BF16), pltpu.VMEM((H_B * D_B, tk), BF16),
         pltpu.VMEM((1, hq), F32), pltpu.VMEM((D_B, hq), F32)])


def _t5_bucket_np(rel):
    nb = NUM_BUCKETS // 2
    max_exact = nb // 2
    ret = np.where(rel > 0, nb, 0)
    n = np.abs(rel)
    nf = np.maximum(n, 1).astype(np.float32)
    large = max_exact + (np.log(nf / max_exact) / math.log(MAX_DISTANCE / max_exact)
                         * (nb - max_exact)).astype(np.int32)
    large = np.minimum(large, nb - 1)
    return (ret + np.where(n < max_exact, n, large)).astype(np.int32)


def _bias_d0(tq):
    return tq // KB - 1


def _bias_kernel(rb_ref, bkt_ref, o_ref):
    bkt = bkt_ref[0]
    tq = bkt.shape[1]
    for h in range(H_C):
        acc = jnp.zeros(bkt.shape, F32)
        for b in range(NUM_BUCKETS):
            acc = jnp.where(bkt == b, rb_ref[b, h], acc)
        o_ref[0, :, h * tq:(h + 1) * tq] = acc


def _bias_tiles(rel_bias, tq, n_off):
    j = np.arange(KB)[None, :, None]
    i = np.arange(tq)[None, None, :]
    d = np.arange(n_off)[:, None, None]
    buckets = jnp.asarray(_t5_bucket_np(j - i - (d - _bias_d0(tq)) * KB))
    return pl.pallas_call(
        _bias_kernel,
        grid=(n_off,),
        in_specs=[pl.BlockSpec(memory_space=pltpu.SMEM),
                  pl.BlockSpec((1, KB, tq), lambda d: (d, 0, 0))],
        out_specs=pl.BlockSpec((1, KB, H_C * tq), lambda d: (d, 0, 0)),
        out_shape=jax.ShapeDtypeStruct((n_off, KB, H_C * tq), F32),
        compiler_params=_cparams("parallel"),
        name="dsa_bias_tiles",
    )(rel_bias, buckets)


def _dsa_kernel(q_ref, qi_ref, k_ref, ki_ref, v_ref, wi_ref, bias_ref, o_ref,
                qm_ref, qim_ref, vt_ref, key_ref, hi_ref, lo_ref, m_ref, l_ref, acc_ref, *, q_off, kv_len):
    tq, tk = q_ref.shape[1], k_ref.shape[1]
    pair = 2 * tq
    qb = pl.program_id(1)
    q0 = q_off + qb * tq
    nk = _num_key_steps(q0, tq, kv_len)
    topk = min(TOPK_MAX, kv_len // 4)

    @pl.when(qb == 0)
    def _():
        _fill_transposed(v_ref, vt_ref)

    for h in range(H_C):
        blk = slice((h // 2) * LANE, (h // 2 + 1) * LANE)
        qm_ref[h * tq:(h + 1) * tq, :] = _head_lanes(q_ref[0, :, blk], D_C, h % 2)
        qim_ref[h * tq:(h + 1) * tq, :] = _head_lanes(qi_ref[0, :, blk], D_IDX, h % 2)

    wi = wi_ref[...]

    def score_step(c, masked):
        k0 = pl.multiple_of(c * KC, KC)
        ki = ki_ref[0, pl.ds(k0, KC), :]
        r = [_dot_nt(ki, qim_ref[g * pair:(g + 1) * pair, :]) for g in range(H_IDX // 2)]
        score = jnp.zeros((KC, tq), F32)
        for h in range(H_IDX):
            score = score + jnp.maximum(r[h // 2][:, (h % 2) * tq:(h % 2 + 1) * tq], 0.0) * wi[h:h + 1, :]
        bits = lax.bitcast_convert_type(score + 0.0, jnp.int32)
        key = bits ^ ((bits >> 31) & 0x7FFFFFFF)
        if masked:
            key = jnp.where(_chunk_visible(*_positions(q0, k0, tq), kv_len, tk), key, INT_MIN)
        key_ref[pl.ds(k0, KC), :] = key
        hi_ref[pl.ds(k0, KC), :] = (key >> 16).astype(jnp.int16)
        return 0

    n_plain = _num_unmasked_key_steps(q0, kv_len)
    lax.fori_loop(0, n_plain, lambda c, _: score_step(c, False), 0)
    lax.fori_loop(n_plain, nk, lambda c, _: score_step(c, True), 0)

    def count(pred):
        def body(c, acc):
            k0 = pl.multiple_of(c * KC, KC)
            hit = jnp.where(pred(key_ref[pl.ds(k0, KC), :]), 1, 0)
            return acc + jnp.sum(hit.reshape(KC // 8, 8, tq), axis=0)
        acc = lax.fori_loop(0, nk, body, jnp.zeros((8, tq), jnp.int32))
        return jnp.sum(acc, axis=0, keepdims=True)

    def count16(ref, cand):
        cand16 = cand.astype(jnp.int16)

        def body(c, acc):
            k0 = pl.multiple_of(c * KC, KC)
            hit = jnp.where(ref[pl.ds(k0, KC), :] >= cand16, jnp.int16(1), jnp.int16(0))
            parts = [hit[r * 16:(r + 1) * 16, :] for r in range(KC // 16)]
            while len(parts) > 1:
                parts = [parts[i] + parts[i + 1] for i in range(0, len(parts), 2)]
            return acc + parts[0]
        acc = lax.fori_loop(0, nk, body, jnp.zeros((16, tq), jnp.int16))
        return jnp.sum(acc.astype(jnp.int32), axis=0, keepdims=True)

    def search16(ref, want):
        def body(i, thr):
            cand = thr + jnp.left_shift(jnp.int32(1), 15 - i)
            return jnp.where(count16(ref, cand) >= want, cand, thr)
        return lax.fori_loop(0, 16, body, jnp.full((1, tq), I16_MIN, jnp.int32))

    thr_hi = search16(hi_ref, topk)
    n_above = jnp.where(thr_hi < I16_MAX, count16(hi_ref, jnp.minimum(thr_hi + 1, I16_MAX)), 0)

    def low_halves(c, carry):
        k0 = pl.multiple_of(c * KC, KC)
        key = key_ref[pl.ds(k0, KC), :]
        low = (key & 0xFFFF) + I16_MIN
        lo_ref[pl.ds(k0, KC), :] = jnp.where((key >> 16) == thr_hi, low, I16_MIN).astype(jnp.int16)
        return carry

    lax.fori_loop(0, nk, low_halves, 0)
    thr_lo = search16(lo_ref, topk - n_above)
    thr = thr_hi * 65536 + (thr_lo - I16_MIN)

    n_ge = count(lambda key: key >= thr)
    surplus = jnp.where(thr > INT_MIN, n_ge - topk, 0)

    @pl.when(jnp.max(surplus) > 0)
    def _():
        need = topk - count(lambda key: key > thr)
        upto = jnp.where(lax.broadcasted_iota(jnp.int32, (KC, KC), 1)
                         <= lax.broadcasted_iota(jnp.int32, (KC, KC), 0), 1.0, 0.0).astype(BF16)

        def body(c, seen):
            k0 = pl.multiple_of(c * KC, KC)
            key = key_ref[pl.ds(k0, KC), :]
            tie = key == thr
            rank = seen + _dot(upto, jnp.where(tie, 1.0, 0.0).astype(BF16)).astype(jnp.int32)
            key_ref[pl.ds(k0, KC), :] = jnp.where(tie & (rank > need), INT_MIN, key)
            return seen + jnp.sum(jnp.where(tie, 1, 0), axis=0, keepdims=True)

        lax.fori_loop(0, nk, body, jnp.zeros((1, tq), jnp.int32))

    thr_sel = jnp.maximum(thr, INT_MIN + 1)
    m_ref[...] = jnp.full(m_ref.shape, NEG, F32)
    l_ref[...] = jnp.zeros(l_ref.shape, F32)
    acc_ref[...] = jnp.zeros(acc_ref.shape, F32)
    d0 = q0 // KB

    def attend(c, carry):
        k0 = pl.multiple_of(c * KC, KC)
        kc = k_ref[0, pl.ds(k0, KC), :]
        vt = vt_ref[:D_C, pl.ds(k0, KC)]
        off = jnp.where(key_ref[pl.ds(k0, KC), :] >= thr_sel, 0.0, NEG)
        off = jnp.concatenate([off, off], axis=1)
        tiles = [jnp.maximum(d0 + _bias_d0(tq) - (KC // KB) * c - r, 0) for r in range(KC // KB)]
        groups = range(H_C // 2)
        cols = [slice(g * pair, (g + 1) * pair) for g in groups]
        s = [_dot_nt(kc, qm_ref[cols[g], :])
             + jnp.concatenate([bias_ref[t, :, cols[g]] for t in tiles], axis=0) + off for g in groups]
        m_prev = [m_ref[:, cols[g]] for g in groups]
        m_new = [jnp.maximum(m_prev[g], jnp.max(s[g], axis=0, keepdims=True)) for g in groups]
        p = [jnp.exp(s[g] - m_new[g]) for g in groups]
        alpha = [jnp.exp(m_prev[g] - m_new[g]) for g in groups]
        pv = [_dot(vt, p[g].astype(BF16)) for g in groups]
        for g in groups:
            l_ref[:, cols[g]] = alpha[g] * l_ref[:, cols[g]] + jnp.sum(p[g], axis=0, keepdims=True)
            acc_ref[:, cols[g]] = alpha[g] * acc_ref[:, cols[g]] + pv[g]
            m_ref[:, cols[g]] = m_new[g]
        return carry

    lax.fori_loop(0, nk, attend, 0)
    o = acc_ref[...] / l_ref[...]
    _store_heads_token_major(o_ref, [o[:, h * tq:(h + 1) * tq] for h in range(H_C)])


def _dsa_attention(q, qi, k2, ki2, v2, wi_t, bias, tq, q_off, kv_len):
    tk = k2.shape[1]
    nq = q.shape[1] // tq
    hq = H_C * tq
    return _attn_call(
        functools.partial(_dsa_kernel, q_off=q_off, kv_len=kv_len), "dsa_attention", tq,
        (q, qi), (k2, ki2, v2),
        [pltpu.VMEM((hq, LANE), BF16), pltpu.VMEM((hq, LANE), BF16), pltpu.VMEM((LANE, tk), BF16),
         pltpu.VMEM((tk, tq), jnp.int32), pltpu.VMEM((tk, tq), jnp.int16), pltpu.VMEM((tk, tq), jnp.int16),
         pltpu.VMEM((1, hq), F32), pltpu.VMEM((1, hq), F32),
         pltpu.VMEM((D_C, hq), F32)],
        extra_in=(wi_t, bias),
        extra_specs=(pl.BlockSpec((WI_ROWS, tq), lambda i, j: (0, i * nq + j)), _const_spec(bias.shape)))


def _merge_kernel(x_ref, oa_ref, ob_ref, oc_ref, g_ref, wg_ref, wbr_ref, wo_ref, post_ref, o_ref):
    x = x_ref[...]
    u = _rms(x, g_ref[...]).astype(BF16)
    merged = jnp.zeros(x.shape, F32)
    for i, o_br in enumerate((oa_ref, ob_ref, oc_ref)):
        gate = jax.nn.sigmoid(_dot(u, wg_ref[:, i * D_MODEL:(i + 1) * D_MODEL]))
        merged = merged + gate * _dot(o_br[...], wbr_ref[i])
    y = _dot(merged.astype(BF16), wo_ref[...])
    o_ref[...] = x + _rms(y, post_ref[...])


def _merge(x, oa, ob, oc, g, w_gate, w_br, w_o, post_g, tm):
    n = x.shape[0]
    row = pl.BlockSpec((tm, D_MODEL), lambda i: (i, 0))
    br = pl.BlockSpec((tm, MIX_W), lambda i: (i, 0))
    return pl.pallas_call(
        _merge_kernel,
        grid=(n // tm,),
        in_specs=[row, br, br, br, _const_spec((1, D_MODEL)), _const_spec(w_gate.shape),
                  _const_spec(w_br.shape), _const_spec(w_o.shape), _const_spec((1, D_MODEL))],
        out_specs=row,
        out_shape=jax.ShapeDtypeStruct((n, D_MODEL), F32),
        compiler_params=_cparams("parallel"),
        name="mixer_merge",
    )(x, oa, ob, oc, g, w_gate, w_br, w_o, post_g)


GATHER_WINDOW = 3 * LANE


def _gather_cols_kernel(start_ref, w_ref, rel_ref, o_ref):
    start = pl.multiple_of(start_ref[pl.program_id(0)], LANE)
    window = w_ref[:, pl.ds(start, GATHER_WINDOW)]
    pick = jnp.where(lax.broadcasted_iota(jnp.int32, (GATHER_WINDOW, LANE), 0) == rel_ref[0], 1.0, 0.0)
    o_ref[...] = _dot(window, pick.astype(BF16)).astype(BF16)


def _gather_cols(w, src):
    src = np.asarray(src, np.int64).reshape(-1, LANE)
    lo = np.where(src >= 0, src, np.iinfo(np.int64).max).min(axis=1)
    lo = np.where(lo == np.iinfo(np.int64).max, 0, lo)
    start = (lo // LANE) * LANE
    rel = np.where(src >= 0, src - start[:, None], -1)
    assert rel.max() < GATHER_WINDOW
    rows, cols = w.shape
    padded = -(-(int(start.max()) + GATHER_WINDOW) // LANE) * LANE
    w = jnp.pad(w.astype(BF16), ((0, 0), (0, max(0, padded - cols))))
    return pl.pallas_call(
        _gather_cols_kernel,
        grid_spec=pltpu.PrefetchScalarGridSpec(
            num_scalar_prefetch=1,
            grid=(src.shape[0],),
            in_specs=[pl.BlockSpec(w.shape, lambda t, start: (0, 0), pipeline_mode=pl.Buffered(1)),
                      pl.BlockSpec((1, 1, LANE), lambda t, start: (t, 0, 0))],
            out_specs=pl.BlockSpec((rows, LANE), lambda t, start: (0, t))),
        out_shape=jax.ShapeDtypeStruct((rows, src.size), BF16),
        compiler_params=_cparams("parallel"),
        name="gather_weight_columns",
    )(jnp.asarray(start, jnp.int32), w, jnp.asarray(rel.reshape(-1, 1, LANE), jnp.int32))


def _in_proj_sources():
    off = dict(zip(('q_lat', 'kv_lat', 'k_r', 'q_b', 'k_b', 'v_b', 'q_c', 'k_c', 'v_c', 'q_i', 'k_i', 'w_i',
                    'g_a', 'g_b', 'g_c'), IN_OFFSETS))

    def piece(name, width):
        return np.arange(off[name], off[name] + width)

    k_r = piece('k_r', A_ROPE)
    k_r_swapped = np.concatenate([k_r[A_ROPE // 2:], k_r[:A_ROPE // 2]])
    pack = np.concatenate(
        [piece('q_lat', Q_LORA), piece('kv_lat', KV_LORA), np.tile(k_r, LANE // A_ROPE),
         np.tile(k_r_swapped, LANE // A_ROPE), piece('q_b', MIX_W), piece('k_b', MIX_W), piece('v_b', MIX_W),
         piece('q_c', MIX_W), piece('q_i', MIX_W), np.tile(piece('k_c', D_C), 2), np.tile(piece('v_c', D_C), 2),
         np.tile(piece('k_i', D_IDX), 2)])
    assert pack.size == _P_END
    return pack, piece('g_a', 3 * D_MODEL)


def _uq_sources():
    head = np.arange(H_A)[:, None] * (A_NOPE + A_ROPE)
    nope = head + np.arange(A_NOPE)[None, :]
    rope = head + A_NOPE + np.arange(A_ROPE)[None, :]
    swapped = np.concatenate([rope[:, A_ROPE // 2:], rope[:, :A_ROPE // 2]], axis=1)
    return np.concatenate([nope.ravel(), rope.ravel(), swapped.ravel()])


def _ukv_sources():
    head = np.arange(H_A)[:, None] * (A_NOPE + A_V)
    return np.concatenate([(head + np.arange(A_NOPE)[None, :]).ravel(),
                           (head + A_NOPE + np.arange(A_V)[None, :]).ravel()])


def _pack_layer(P, l):
    w_in = P['w_in'][l]
    src_pack, src_gate = _in_proj_sources()
    w_pack = _gather_cols(w_in, src_pack)
    w_gate = _gather_cols(w_in, src_gate)
    w_i = w_in[:, IN_OFFSETS[11]:IN_OFFSETS[12]]
    wwi_t = jnp.pad(w_i.T, ((0, WI_ROWS - H_IDX), (0, 0))).astype(BF16)
    wuq_pack = _gather_cols(P['w_mla_uq'][l], _uq_sources())
    wukv_pack = _gather_cols(P['w_mla_ukv'][l], _ukv_sources())
    return dict(
        w_pack=w_pack, wwi_t=wwi_t, wuq_pack=wuq_pack, wukv_pack=wukv_pack,
        w_gate=w_gate,
        w_br=P['w_branch'][l].astype(BF16), w_o=P['w_o'][l].astype(BF16),
        mix_pre=P['mix_pre_gain'][l][None], mix_post=P['mix_post_gain'][l][None],
        q_norm=P['mla_q_norm'][l][None], kv_norm=P['mla_kv_norm'][l][None],
        ffn=[dict(pre=P['ffn_pre_gain'][l, i][None], post=P['ffn_post_gain'][l, i][None],
                  w_up=P['w_ffn_up'][l, i].astype(BF16), w_down=P['w_ffn_down'][l, i].astype(BF16))
             for i in range(2)])


def _rope_tables(pos, rows):
    half = A_ROPE // 2
    inv = ROPE_THETA ** (-jnp.arange(half, dtype=F32) / half)
    ang = pos.astype(F32)[:, None] * inv[None, :]
    c, s = jnp.cos(ang), jnp.sin(ang)
    ck = jnp.concatenate([c, c], axis=1)
    sk = jnp.concatenate([-s, s], axis=1)
    rep = max(1, rows // pos.shape[0])
    ck, sk = jnp.tile(ck, (rep, 1)), jnp.tile(sk, (rep, 1))
    return (jnp.tile(ck, (1, H_A)), jnp.tile(sk, (1, H_A)),
            jnp.tile(ck, (1, LANE // A_ROPE)), jnp.tile(sk, (1, LANE // A_ROPE)))


def _pad_axis1(a, size):
    if a.shape[1] == size:
        return a
    pad = [(0, 0)] * a.ndim
    pad[1] = (0, size - a.shape[1])
    return jnp.pad(a, pad)


def _mixer(x, L, past, bias_tiles, b, t, tm, tq):
    n = b * t
    p_len = 0 if past is None else past[0].shape[1]
    kv_len = p_len + t
    tk = -(-kv_len // KC) * KC
    tqp = -(-t // tq) * tq
    pos = p_len + jnp.arange(t, dtype=jnp.int32)
    cq, sq, ck, sk = _rope_tables(pos, tm)
    (ckv, krope, kb, vb, kc, vc, ki, wi_t, qan, qar, qb, qc, qi, kr4, kb16, vb16, kc2, vc2, ki2) = _mixer_in(
        x, L['mix_pre'], L['w_pack'], L['wwi_t'], L['q_norm'], L['wuq_pack'], L['kv_norm'], cq, sq, ck, sk, tm)
    new_rows = (ckv.reshape(b, t, KV_LORA), krope.reshape(b, t, A_ROPE),
                kb.reshape(b, t, H_B, D_B), vb.reshape(b, t, H_B, D_B),
                kc.reshape(b, t, D_C), vc.reshape(b, t, D_C), ki.reshape(b, t, D_IDX))

    def keys(a):
        return _pad_axis1(a, tk)

    if past is None:
        ckv_all = keys(new_rows[0])
        kr4_all, kb_all, vb_all, kc_all, vc_all, ki_all = (
            keys(a.reshape(b, t, -1)) for a in (kr4, kb16, vb16, kc2, vc2, ki2))
    else:
        def with_past(pa, nr, w):
            return keys(jnp.concatenate([pa.reshape(b, p_len, w), nr.reshape(b, t, w)], axis=1))

        def lanes(a, rep):
            return jnp.tile(a.astype(BF16), (1, 1, rep))

        ckv_all = with_past(past[0], ckv, KV_LORA)
        kr4_all = lanes(with_past(past[1], krope, A_ROPE), LANE // A_ROPE)
        kb_all = with_past(past[2], kb, MIX_W).astype(BF16)
        vb_all = with_past(past[3], vb, MIX_W).astype(BF16)
        kc_all, vc_all, ki_all = (lanes(with_past(pa, nr, w), LANE // w)
                                  for pa, nr, w in ((past[4], kc, D_C), (past[5], vc, D_C), (past[6], ki, D_IDX)))

    def queries(a):
        return _pad_axis1(a.reshape(b, t, -1), tqp)

    def attended(o):
        return o[:, :t].reshape(n, MIX_W)

    k_nope, v_a = _ukv(ckv_all.reshape(b * tk, KV_LORA), L['wukv_pack'], 1024 if (b * tk) % 1024 == 0 else tk)
    o_a = _mla_attention(queries(qan), queries(qar), k_nope.reshape(b, tk, MIX_W), kr4_all,
                         v_a.reshape(b, tk, MIX_W), tq, p_len, kv_len)
    o_b = _sb_attention(queries(qb), kb_all, vb_all, tq, p_len, kv_len)
    wi_t = _pad_axis1(wi_t.reshape(WI_ROWS * b, t), tqp).reshape(WI_ROWS, b * tqp)
    o_c = _dsa_attention(queries(qc), queries(qi), kc_all, ki_all, vc_all, wi_t, bias_tiles, tq, p_len, kv_len)
    x = _merge(x, attended(o_a), attended(o_b), attended(o_c), L['mix_pre'], L['w_gate'], L['w_br'], L['w_o'],
               L['mix_post'], tm)
    return x, new_rows


def _trunk(x, past, layers, rel_bias):
    b, t, _ = x.shape
    n = b * t
    tm = min(512, n)
    p_len = 0 if past is None else past[0][0].shape[1]
    tq = min(QUERY_TILE_CAP, -(-t // KB) * KB)
    assert p_len % KB == 0 and tq % KB == 0
    n_off = (p_len + (-(-t // tq) - 1) * tq) // KB + 1 + _bias_d0(tq)
    bias_tiles = _bias_tiles(rel_bias, tq, n_off)
    x = x.reshape(n, D_MODEL)
    rows = []
    for l, L in enumerate(layers):
        f = L['ffn'][0]
        x = _ffn_half(x, f['pre'], f['w_up'], f['w_down'], f['post'], tm)
        x, new = _mixer(x, L, None if past is None else past[l], bias_tiles, b, t, tm, tq)
        f = L['ffn'][1]
        x = _ffn_half(x, f['pre'], f['w_up'], f['w_down'], f['post'], tm)
        rows.append(new)
    return x.reshape(b, t, D_MODEL), tuple(jnp.stack(g) for g in zip(*rows))


def kernel(x_prompt, x_sample, cache_mla_ckv, cache_mla_krope, cache_sb_k, cache_sb_v, cache_dsa_k, cache_dsa_v, cache_dsa_kidx, rel_bias, ffn_pre_gain, w_ffn_up, w_ffn_down, ffn_post_gain, mix_pre_gain, w_in, mla_q_norm, w_mla_uq, mla_kv_norm, w_mla_ukv, w_branch, w_o, mix_post_gain):
    P = dict(ffn_pre_gain=ffn_pre_gain, w_ffn_up=w_ffn_up, w_ffn_down=w_ffn_down,
             ffn_post_gain=ffn_post_gain, mix_pre_gain=mix_pre_gain, w_in=w_in,
             mla_q_norm=mla_q_norm, w_mla_uq=w_mla_uq, mla_kv_norm=mla_kv_norm,
             w_mla_ukv=w_mla_ukv, w_branch=w_branch, w_o=w_o, mix_post_gain=mix_post_gain)
    depth = w_in.shape[0]
    layers = [_pack_layer(P, l) for l in range(depth)]
    y_prompt, p_rows = _trunk(x_prompt, None, layers, rel_bias)
    caches = (cache_mla_ckv, cache_mla_krope, cache_sb_k, cache_sb_v,
              cache_dsa_k, cache_dsa_v, cache_dsa_kidx)
    past = [tuple(c[l] for c in caches) for l in range(depth)]
    y_sample, s_rows = _trunk(x_sample, past, layers, rel_bias)
    return (y_prompt, y_sample) + p_rows + s_rows
```

```python
import functools
import math

import numpy as np
import jax
import jax.numpy as jnp
from jax import lax
from jax.experimental import pallas as pl
from jax.experimental.pallas import tpu as pltpu

D_MODEL = 1024
CHUNK = 64
CHUNK_SHIFT = 6
EPS = 1e-6
H_A, A_NOPE, A_ROPE, A_V = 8, 64, 32, 64
Q_LORA, KV_LORA = 384, 256
ROPE_THETA = 10000.0
H_B, D_B = 8, 64
H_C, D_C = 8, 64
H_IDX, D_IDX = 8, 64
TOPK_MAX = 256
NUM_BUCKETS, MAX_DISTANCE = 32, 256
MIX_W = 512
D_FF = 2816
IN_SPLITS = (Q_LORA, KV_LORA, A_ROPE, H_B * D_B, H_B * D_B, H_B * D_B,
             H_C * D_C, D_C, D_C, H_IDX * D_IDX, D_IDX, H_IDX,
             D_MODEL, D_MODEL, D_MODEL)
IN_OFFSETS = tuple(int(o) for o in np.cumsum((0,) + IN_SPLITS))

LANE = 128
KC = 256
QUERY_TILE_CAP = 256
KB = 128
FF_CHUNK = 256
NEG = -1e30
INT_MIN = -2 ** 31
I16_MIN, I16_MAX = -2 ** 15, 2 ** 15 - 1
VMEM_BYTES_V7X = 64 * 1024 * 1024
VMEM_LIMIT = VMEM_BYTES_V7X - 8 * 1024 * 1024
BF16 = jnp.bfloat16
F32 = jnp.float32


def _cparams(*sem):
    return pltpu.CompilerParams(dimension_semantics=sem, vmem_limit_bytes=VMEM_LIMIT)


def _const_spec(shape):
    nd = len(shape)
    return pl.BlockSpec(shape, lambda *_: (0,) * nd, pipeline_mode=pl.Buffered(1))


def _rms(x, g):
    return x * lax.rsqrt(jnp.mean(x * x, axis=-1, keepdims=True) + EPS) * g


def _dot(a, b):
    return jnp.dot(a, b, preferred_element_type=F32)


def _dot_nt(a, b):
    return lax.dot_general(a, b, (((1,), (1,)), ((), ())), preferred_element_type=F32)


def _ffn_kernel(x_ref, pre_ref, wup_ref, wdn_ref, post_ref, o_ref):
    x = x_ref[...]
    h = _rms(x, pre_ref[...]).astype(BF16)
    acc = jnp.zeros(x.shape, F32)
    for c in range(D_FF // FF_CHUNK):
        lo = c * FF_CHUNK
        gate = _dot(h, wup_ref[:, lo:lo + FF_CHUNK])
        up = _dot(h, wup_ref[:, D_FF + lo:D_FF + lo + FF_CHUNK])
        act = (gate * jax.nn.sigmoid(gate) * up).astype(BF16)
        acc = acc + _dot(act, wdn_ref[lo:lo + FF_CHUNK, :])
    o_ref[...] = x + 0.5 * _rms(acc, post_ref[...])


def _ffn_half(x, pre_g, w_up, w_down, post_g, tm):
    n = x.shape[0]
    row = pl.BlockSpec((tm, D_MODEL), lambda i: (i, 0))
    return pl.pallas_call(
        _ffn_kernel,
        grid=(n // tm,),
        in_specs=[row, _const_spec((1, D_MODEL)), _const_spec((D_MODEL, 2 * D_FF)),
                  _const_spec((D_FF, D_MODEL)), _const_spec((1, D_MODEL))],
        out_specs=row,
        out_shape=jax.ShapeDtypeStruct((n, D_MODEL), F32),
        compiler_params=_cparams("parallel"),
        name="ffn_half",
    )(x, pre_g, w_up, w_down, post_g)


_P_QLAT = 0
_P_KVLAT = _P_QLAT + Q_LORA
_P_KR = _P_KVLAT + KV_LORA
_P_KRSW = _P_KR + LANE
_P_QB = _P_KRSW + LANE
_P_KB = _P_QB + MIX_W
_P_VB = _P_KB + MIX_W
_P_QC = _P_VB + MIX_W
_P_QI = _P_QC + MIX_W
_P_KC = _P_QI + MIX_W
_P_VC = _P_KC + LANE
_P_KI = _P_VC + LANE
_P_END = _P_KI + LANE


WI_ROWS = 16


def _in_kernel(x_ref, g_ref, w_ref, wwi_ref, qn_ref, wuq_ref, kvn_ref, cq_ref, sq_ref, ck_ref, sk_ref,
               ckv_o, krope_o, kb_o, vb_o, kc_o, vc_o, ki_o, wit_o,
               qan_o, qar_o, qb_o, qc_o, qi_o, kr4_o, kb16_o, vb16_o, kc2_o, vc2_o, ki2_o):
    u = _rms(x_ref[...], g_ref[...]).astype(BF16)
    p = _dot(u, w_ref[...])
    c_q = _rms(p[:, _P_QLAT:_P_QLAT + Q_LORA], qn_ref[...]).astype(BF16)
    qa = _dot(c_q, wuq_ref[...])
    n_nope, n_rope = H_A * A_NOPE, H_A * A_ROPE
    scale_a = (A_NOPE + A_ROPE) ** -0.5
    qan_o[...] = (qa[:, :n_nope] * scale_a).astype(BF16)
    q_rot = qa[:, n_nope:n_nope + n_rope] * cq_ref[...] + qa[:, n_nope + n_rope:] * sq_ref[...]
    qar_o[...] = (q_rot * scale_a).astype(BF16)
    ckv_o[...] = _rms(p[:, _P_KVLAT:_P_KVLAT + KV_LORA], kvn_ref[...])
    k_rot = p[:, _P_KR:_P_KR + LANE] * ck_ref[...] + p[:, _P_KRSW:_P_KRSW + LANE] * sk_ref[...]
    krope_o[...] = k_rot[:, :A_ROPE]
    kr4_o[...] = k_rot.astype(BF16)
    qb_o[...] = (p[:, _P_QB:_P_QB + MIX_W] * D_B ** -0.5).astype(BF16)
    k_b, v_b = p[:, _P_KB:_P_KB + MIX_W], p[:, _P_VB:_P_VB + MIX_W]
    kb_o[...] = k_b
    vb_o[...] = v_b
    kb16_o[...] = k_b.astype(BF16)
    vb16_o[...] = v_b.astype(BF16)
    qc_o[...] = (p[:, _P_QC:_P_QC + MIX_W] * D_C ** -0.5).astype(BF16)
    qi_o[...] = (p[:, _P_QI:_P_QI + MIX_W] * D_IDX ** -0.5).astype(BF16)
    for at, f32_o, bf16_o in ((_P_KC, kc_o, kc2_o), (_P_VC, vc_o, vc2_o), (_P_KI, ki_o, ki2_o)):
        blk = p[:, at:at + LANE]
        f32_o[...] = blk[:, :LANE // 2]
        bf16_o[...] = blk.astype(BF16)
    wit_o[...] = _dot_nt(wwi_ref[...], u) * H_IDX ** -0.5


def _mixer_in(x, g, w_pack, wwi_t, qn, wuq_pack, kvn, cq, sq, ck, sk, tm):
    n = x.shape[0]
    n_tab = cq.shape[0] // tm

    def row(w):
        return pl.BlockSpec((tm, w), lambda i: (i, 0))

    def tab(w):
        return pl.BlockSpec((tm, w), lambda i: (i % n_tab, 0))

    widths_f32 = (KV_LORA, A_ROPE, MIX_W, MIX_W, D_C, D_C, D_IDX)
    widths_bf16 = (H_A * A_NOPE, H_A * A_ROPE, MIX_W, MIX_W, MIX_W, LANE, MIX_W, MIX_W, LANE, LANE, LANE)
    return pl.pallas_call(
        _in_kernel,
        grid=(n // tm,),
        in_specs=[row(D_MODEL), _const_spec((1, D_MODEL)), _const_spec(w_pack.shape), _const_spec(wwi_t.shape),
                  _const_spec((1, Q_LORA)), _const_spec(wuq_pack.shape), _const_spec((1, KV_LORA)),
                  tab(H_A * A_ROPE), tab(H_A * A_ROPE), tab(LANE), tab(LANE)],
        out_specs=([row(w) for w in widths_f32] + [pl.BlockSpec((WI_ROWS, tm), lambda i: (0, i))]
                   + [row(w) for w in widths_bf16]),
        out_shape=([jax.ShapeDtypeStruct((n, w), F32) for w in widths_f32]
                   + [jax.ShapeDtypeStruct((WI_ROWS, n), F32)]
                   + [jax.ShapeDtypeStruct((n, w), BF16) for w in widths_bf16]),
        compiler_params=_cparams("parallel"),
        name="mixer_in",
    )(x, g, w_pack, wwi_t, qn, wuq_pack, kvn, cq, sq, ck, sk)


def _ukv_kernel(c_ref, w_ref, k_o, v_o):
    kv = _dot(c_ref[...].astype(BF16), w_ref[...])
    k_o[...] = kv[:, :H_A * A_NOPE].astype(BF16)
    v_o[...] = kv[:, H_A * A_NOPE:].astype(BF16)


def _ukv(ckv, w_pack, tm):
    n = ckv.shape[0]
    return pl.pallas_call(
        _ukv_kernel,
        grid=(n // tm,),
        in_specs=[pl.BlockSpec((tm, KV_LORA), lambda i: (i, 0)), _const_spec(w_pack.shape)],
        out_specs=[pl.BlockSpec((tm, H_A * A_NOPE), lambda i: (i, 0)),
                   pl.BlockSpec((tm, H_A * A_V), lambda i: (i, 0))],
        out_shape=[jax.ShapeDtypeStruct((n, H_A * A_NOPE), BF16),
                   jax.ShapeDtypeStruct((n, H_A * A_V), BF16)],
        compiler_params=_cparams("parallel"),
        name="mla_ukv",
    )(ckv, w_pack)


def _num_key_steps(q0, tq, kv_len):
    k_end = jnp.minimum(kv_len, ((q0 + tq - 1) // CHUNK + 1) * CHUNK)
    return (k_end + KC - 1) // KC


def _num_unmasked_key_steps(q0, kv_len):
    return jnp.minimum(kv_len, (q0 // CHUNK + 1) * CHUNK) // KC


def _positions(q0, k0, tq):
    kpos = k0 + lax.broadcasted_iota(jnp.int32, (KC, tq), 0)
    qpos = q0 + lax.broadcasted_iota(jnp.int32, (KC, tq), 1)
    return kpos, qpos


def _chunk_visible(kpos, qpos, kv_len, tk):
    vis = (kpos >> CHUNK_SHIFT) <= (qpos >> CHUNK_SHIFT)
    return vis & (kpos < kv_len) if kv_len < tk else vis


def _head_lanes(x, width, idx):
    lane = lax.broadcasted_iota(jnp.int32, (1, x.shape[1]), 1) & (LANE - 1)
    return x * jnp.where((lane >> (width.bit_length() - 1)) == idx, 1.0, 0.0).astype(x.dtype)


def _fill_transposed(src_ref, dst_ref):
    tk, width = src_ref.shape[1], src_ref.shape[2]

    def body(c, carry):
        k0 = pl.multiple_of(c * KC, KC)
        blk = src_ref[0, pl.ds(k0, KC), :].astype(F32)
        for g in range(width // LANE):
            dst_ref[g * LANE:(g + 1) * LANE, pl.ds(k0, KC)] = blk[:, g * LANE:(g + 1) * LANE].T.astype(dst_ref.dtype)
        return carry

    lax.fori_loop(0, tk // KC, body, 0)


def _store_heads_token_major(o_ref, heads):
    for g in range(len(heads) // 2):
        pair = jnp.concatenate([heads[2 * g], heads[2 * g + 1]], axis=0)
        o_ref[0, :, g * LANE:(g + 1) * LANE] = pair.T.astype(o_ref.dtype)


def _attn_call(kernel_fn, name, tq, q_arrays, k_arrays, scratch, extra_in=(), extra_specs=()):
    b, t = q_arrays[0].shape[:2]
    in_specs = ([pl.BlockSpec((1, tq, a.shape[2]), lambda i, j: (i, j, 0)) for a in q_arrays]
                + [pl.BlockSpec((1,) + a.shape[1:], lambda i, j: (i, 0, 0)) for a in k_arrays]
                + list(extra_specs))
    return pl.pallas_call(
        kernel_fn,
        grid=(b, t // tq),
        in_specs=in_specs,
        out_specs=pl.BlockSpec((1, tq, MIX_W), lambda i, j: (i, j, 0)),
        out_shape=jax.ShapeDtypeStruct((b, t, MIX_W), BF16),
        scratch_shapes=scratch,
        compiler_params=_cparams("parallel", "arbitrary"),
        name=name,
    )(*q_arrays, *k_arrays, *extra_in)


def _mla_kernel(qn_ref, qr_ref, kn_ref, kr_ref, v_ref, o_ref, qcat_ref, vt_ref, m_ref, l_ref, acc_ref,
                *, q_off, kv_len):
    tq, tk = qn_ref.shape[1], kn_ref.shape[1]
    pair = 2 * tq
    qb = pl.program_id(1)
    q0 = q_off + qb * tq
    nk = _num_key_steps(q0, tq, kv_len)

    @pl.when(qb == 0)
    def _():
        _fill_transposed(v_ref, vt_ref)

    for h in range(H_A):
        rows = slice(h * tq, (h + 1) * tq)
        qcat_ref[rows, :LANE] = _head_lanes(qn_ref[0, :, (h // 2) * LANE:(h // 2 + 1) * LANE], A_NOPE, h % 2)
        qcat_ref[rows, LANE:] = _head_lanes(qr_ref[0, :, (h // 4) * LANE:(h // 4 + 1) * LANE], A_ROPE, h % 4)
    m_ref[...] = jnp.full(m_ref.shape, NEG, F32)
    l_ref[...] = jnp.zeros(l_ref.shape, F32)
    acc_ref[...] = jnp.zeros(acc_ref.shape, F32)
    pairs = range(H_A // 2)
    cols = [slice(g * pair, (g + 1) * pair) for g in pairs]

    def step(c, masked):
        k0 = pl.multiple_of(c * KC, KC)
        kr = kr_ref[0, pl.ds(k0, KC), :]
        s = [_dot_nt(jnp.concatenate([kn_ref[0, pl.ds(k0, KC), g * LANE:(g + 1) * LANE], kr], axis=1),
                     qcat_ref[cols[g], :]) for g in pairs]
        if masked:
            off = jnp.where(_chunk_visible(*_positions(q0, k0, tq), kv_len, tk), 0.0, NEG)
            off = jnp.concatenate([off, off], axis=1)
            s = [s[g] + off for g in pairs]
        m_prev = [m_ref[:, cols[g]] for g in pairs]
        m_new = [jnp.maximum(m_prev[g], jnp.max(s[g], axis=0, keepdims=True)) for g in pairs]
        p = [jnp.exp(s[g] - m_new[g]) for g in pairs]
        alpha = [jnp.exp(m_prev[g] - m_new[g]) for g in pairs]
        p16 = [p[g].astype(BF16) for g in pairs]
        pv = [_dot(vt_ref[h * A_V:(h + 1) * A_V, pl.ds(k0, KC)], p16[h // 2][:, (h % 2) * tq:(h % 2 + 1) * tq])
              for h in range(H_A)]
        for g in pairs:
            l_ref[:, cols[g]] = alpha[g] * l_ref[:, cols[g]] + jnp.sum(p[g], axis=0, keepdims=True)
            acc_ref[:, cols[g]] = (alpha[g] * acc_ref[:, cols[g]]
                                   + jnp.concatenate([pv[2 * g], pv[2 * g + 1]], axis=1))
            m_ref[:, cols[g]] = m_new[g]
        return 0

    n_plain = _num_unmasked_key_steps(q0, kv_len)
    lax.fori_loop(0, n_plain, lambda c, _: step(c, False), 0)
    lax.fori_loop(n_plain, nk, lambda c, _: step(c, True), 0)
    o = acc_ref[...] / l_ref[...]
    _store_heads_token_major(o_ref, [o[:, h * tq:(h + 1) * tq] for h in range(H_A)])


def _mla_attention(qn, qr, kn, kr4, v, tq, q_off, kv_len):
    tk = kn.shape[1]
    hq = H_A * tq
    return _attn_call(
        functools.partial(_mla_kernel, q_off=q_off, kv_len=kv_len), "mla_attention", tq,
        (qn, qr), (kn, kr4, v),
        [pltpu.VMEM((hq, 2 * LANE), BF16), pltpu.VMEM((H_A * A_V, tk), BF16),
         pltpu.VMEM((1, hq), F32), pltpu.VMEM((1, hq), F32), pltpu.VMEM((A_V, hq), F32)])


def _sb_kernel(q_ref, k_ref, v_ref, o_ref, qm_ref, vt_ref, run_ref, acc_ref, *, q_off, kv_len):
    tq = q_ref.shape[1]
    pair = 2 * tq
    qb = pl.program_id(1)
    q0 = q_off + qb * tq
    nk = _num_key_steps(q0, tq, kv_len)

    @pl.when(qb == 0)
    def _():
        _fill_transposed(v_ref, vt_ref)

    for h in range(H_B):
        qm_ref[h * tq:(h + 1) * tq, :] = _head_lanes(q_ref[0, :, (h // 2) * LANE:(h // 2 + 1) * LANE], D_B, h % 2)
    run_ref[...] = jnp.zeros(run_ref.shape, F32)
    acc_ref[...] = jnp.zeros(acc_ref.shape, F32)
    later = jnp.where(lax.broadcasted_iota(jnp.int32, (KC, KC), 1)
                      > lax.broadcasted_iota(jnp.int32, (KC, KC), 0), 1.0, 0.0).astype(BF16)
    pairs = range(H_B // 2)
    cols = [slice(g * pair, (g + 1) * pair) for g in pairs]

    def step(c, masked):
        k0 = pl.multiple_of(c * KC, KC)
        z = [_dot_nt(k_ref[0, pl.ds(k0, KC), g * LANE:(g + 1) * LANE], qm_ref[cols[g], :]) for g in pairs]
        soft = [jnp.log(1.0 + jnp.exp(-jnp.abs(z[g]))) for g in pairs]
        sp = [jnp.maximum(z[g], 0.0) + soft[g] for g in pairs]
        log_beta = [z[g] - sp[g] for g in pairs]
        if masked:
            kpos, qpos = _positions(q0, k0, tq)
            keep = jnp.where(kpos < qpos, 1.0, 0.0)
            keep = jnp.concatenate([keep, keep], axis=1)
            sp = [sp[g] * keep for g in pairs]
        hi = [sp[g].astype(BF16) for g in pairs]
        lo = [(sp[g] - hi[g].astype(F32)).astype(BF16) for g in pairs]
        stick = [_dot(later, hi[g]) + _dot(later, lo[g]) + run_ref[:, cols[g]] for g in pairs]
        a = [jnp.exp(log_beta[g] - stick[g]) for g in pairs]
        if masked:
            a = [a[g] * keep for g in pairs]
        a16 = [a[g].astype(BF16) for g in pairs]
        av = [_dot(vt_ref[h * D_B:(h + 1) * D_B, pl.ds(k0, KC)], a16[h // 2][:, (h % 2) * tq:(h % 2 + 1) * tq])
              for h in range(H_B)]
        for g in pairs:
            acc_ref[:, cols[g]] = acc_ref[:, cols[g]] + jnp.concatenate([av[2 * g], av[2 * g + 1]], axis=1)
            run_ref[:, cols[g]] = run_ref[:, cols[g]] + jnp.sum(sp[g], axis=0, keepdims=True)
        return 0

    n_plain = jnp.minimum(q0, kv_len) // KC
    lax.fori_loop(0, nk - n_plain, lambda i, _: step(nk - 1 - i, True), 0)
    lax.fori_loop(0, n_plain, lambda i, _: step(n_plain - 1 - i, False), 0)
    o = acc_ref[...]
    _store_heads_token_major(o_ref, [o[:, h * tq:(h + 1) * tq] for h in range(H_B)])


def _sb_attention(q, k, v, tq, q_off, kv_len):
    tk = k.shape[1]
    hq = H_B * tq
    return _attn_call(
        functools.partial(_sb_kernel, q_off=q_off, kv_len=kv_len), "sb_attention", tq,
        (q,), (k, v),
        [pltpu.VMEM((hq, LANE), BF16), pltpu.VMEM((H_B * D_B, tk), BF16),
         pltpu.VMEM((1, hq), F32), pltpu.VMEM((D_B, hq), F32)])


def _t5_bucket_np(rel):
    nb = NUM_BUCKETS // 2
    max_exact = nb // 2
    ret = np.where(rel > 0, nb, 0)
    n = np.abs(rel)
    nf = np.maximum(n, 1).astype(np.float32)
    large = max_exact + (np.log(nf / max_exact) / math.log(MAX_DISTANCE / max_exact)
                         * (nb - max_exact)).astype(np.int32)
    large = np.minimum(large, nb - 1)
    return (ret + np.where(n < max_exact, n, large)).astype(np.int32)


def _bias_d0(tq):
    return tq // KB - 1


def _bias_kernel(rb_ref, bkt_ref, o_ref):
    bkt = bkt_ref[0]
    tq = bkt.shape[1]
    for h in range(H_C):
        acc = jnp.zeros(bkt.shape, F32)
        for b in range(NUM_BUCKETS):
            acc = jnp.where(bkt == b, rb_ref[b, h], acc)
        o_ref[0, :, h * tq:(h + 1) * tq] = acc


def _bias_tiles(rel_bias, tq, n_off):
    j = np.arange(KB)[None, :, None]
    i = np.arange(tq)[None, None, :]
    d = np.arange(n_off)[:, None, None]
    buckets = jnp.asarray(_t5_bucket_np(j - i - (d - _bias_d0(tq)) * KB))
    return pl.pallas_call(
        _bias_kernel,
        grid=(n_off,),
        in_specs=[pl.BlockSpec(memory_space=pltpu.SMEM),
                  pl.BlockSpec((1, KB, tq), lambda d: (d, 0, 0))],
        out_specs=pl.BlockSpec((1, KB, H_C * tq), lambda d: (d, 0, 0)),
        out_shape=jax.ShapeDtypeStruct((n_off, KB, H_C * tq), F32),
        compiler_params=_cparams("parallel"),
        name="dsa_bias_tiles",
    )(rel_bias, buckets)


def _dsa_kernel(q_ref, qi_ref, k_ref, ki_ref, v_ref, wi_ref, bias_ref, o_ref,
                qm_ref, qim_ref, vt_ref, key_ref, hi_ref, lo_ref, m_ref, l_ref, acc_ref, *, q_off, kv_len):
    tq, tk = q_ref.shape[1], k_ref.shape[1]
    pair = 2 * tq
    qb = pl.program_id(1)
    q0 = q_off + qb * tq
    nk = _num_key_steps(q0, tq, kv_len)
    topk = min(TOPK_MAX, kv_len // 4)

    @pl.when(qb == 0)
    def _():
        _fill_transposed(v_ref, vt_ref)

    for h in range(H_C):
        blk = slice((h // 2) * LANE, (h // 2 + 1) * LANE)
        qm_ref[h * tq:(h + 1) * tq, :] = _head_lanes(q_ref[0, :, blk], D_C, h % 2)
        qim_ref[h * tq:(h + 1) * tq, :] = _head_lanes(qi_ref[0, :, blk], D_IDX, h % 2)

    wi = wi_ref[...]

    def score_step(c, masked):
        k0 = pl.multiple_of(c * KC, KC)
        ki = ki_ref[0, pl.ds(k0, KC), :]
        r = [_dot_nt(ki, qim_ref[g * pair:(g + 1) * pair, :]) for g in range(H_IDX // 2)]
        score = jnp.zeros((KC, tq), F32)
        for h in range(H_IDX):
            score = score + jnp.maximum(r[h // 2][:, (h % 2) * tq:(h % 2 + 1) * tq], 0.0) * wi[h:h + 1, :]
        bits = lax.bitcast_convert_type(score + 0.0, jnp.int32)
        key = bits ^ ((bits >> 31) & 0x7FFFFFFF)
        if masked:
            key = jnp.where(_chunk_visible(*_positions(q0, k0, tq), kv_len, tk), key, INT_MIN)
        key_ref[pl.ds(k0, KC), :] = key
        hi_ref[pl.ds(k0, KC), :] = (key >> 16).astype(jnp.int16)
        return 0

    n_plain = _num_unmasked_key_steps(q0, kv_len)
    lax.fori_loop(0, n_plain, lambda c, _: score_step(c, False), 0)
    lax.fori_loop(n_plain, nk, lambda c, _: score_step(c, True), 0)

    def count(pred):
        def body(c, acc):
            k0 = pl.multiple_of(c * KC, KC)
            hit = jnp.where(pred(key_ref[pl.ds(k0, KC), :]), 1, 0)
            return acc + jnp.sum(hit.reshape(KC // 8, 8, tq), axis=0)
        acc = lax.fori_loop(0, nk, body, jnp.zeros((8, tq), jnp.int32))
        return jnp.sum(acc, axis=0, keepdims=True)

    def count16(ref, cand):
        cand16 = cand.astype(jnp.int16)

        def body(c, acc):
            k0 = pl.multiple_of(c * KC, KC)
            hit = jnp.where(ref[pl.ds(k0, KC), :] >= cand16, jnp.int16(1), jnp.int16(0))
            parts = [hit[r * 16:(r + 1) * 16, :] for r in range(KC // 16)]
            while len(parts) > 1:
                parts = [parts[i] + parts[i + 1] for i in range(0, len(parts), 2)]
            return acc + parts[0]
        acc = lax.fori_loop(0, nk, body, jnp.zeros((16, tq), jnp.int16))
        return jnp.sum(acc.astype(jnp.int32), axis=0, keepdims=True)

    def search16(ref, want):
        def body(i, thr):
            cand = thr + jnp.left_shift(jnp.int32(1), 15 - i)
            return jnp.where(count16(ref, cand) >= want, cand, thr)
        return lax.fori_loop(0, 16, body, jnp.full((1, tq), I16_MIN, jnp.int32))

    thr_hi = search16(hi_ref, topk)
    n_above = jnp.where(thr_hi < I16_MAX, count16(hi_ref, jnp.minimum(thr_hi + 1, I16_MAX)), 0)

    def low_halves(c, carry):
        k0 = pl.multiple_of(c * KC, KC)
        key = key_ref[pl.ds(k0, KC), :]
        low = (key & 0xFFFF) + I16_MIN
        lo_ref[pl.ds(k0, KC), :] = jnp.where((key >> 16) == thr_hi, low, I16_MIN).astype(jnp.int16)
        return carry

    lax.fori_loop(0, nk, low_halves, 0)
    thr_lo = search16(lo_ref, topk - n_above)
    thr = thr_hi * 65536 + (thr_lo - I16_MIN)

    n_ge = count(lambda key: key >= thr)
    surplus = jnp.where(thr > INT_MIN, n_ge - topk, 0)

    @pl.when(jnp.max(surplus) > 0)
    def _():
        need = topk - count(lambda key: key > thr)
        upto = jnp.where(lax.broadcasted_iota(jnp.int32, (KC, KC), 1)
                         <= lax.broadcasted_iota(jnp.int32, (KC, KC), 0), 1.0, 0.0).astype(BF16)

        def body(c, seen):
            k0 = pl.multiple_of(c * KC, KC)
            key = key_ref[pl.ds(k0, KC), :]
            tie = key == thr
            rank = seen + _dot(upto, jnp.where(tie, 1.0, 0.0).astype(BF16)).astype(jnp.int32)
            key_ref[pl.ds(k0, KC), :] = jnp.where(tie & (rank > need), INT_MIN, key)
            return seen + jnp.sum(jnp.where(tie, 1, 0), axis=0, keepdims=True)

        lax.fori_loop(0, nk, body, jnp.zeros((1, tq), jnp.int32))

    thr_sel = jnp.maximum(thr, INT_MIN + 1)
    m_ref[...] = jnp.full(m_ref.shape, NEG, F32)
    l_ref[...] = jnp.zeros(l_ref.shape, F32)
    acc_ref[...] = jnp.zeros(acc_ref.shape, F32)
    d0 = q0 // KB

    def attend(c, carry):
        k0 = pl.multiple_of(c * KC, KC)
        kc = k_ref[0, pl.ds(k0, KC), :]
        vt = vt_ref[:D_C, pl.ds(k0, KC)]
        off = jnp.where(key_ref[pl.ds(k0, KC), :] >= thr_sel, 0.0, NEG)
        off = jnp.concatenate([off, off], axis=1)
        tiles = [jnp.maximum(d0 + _bias_d0(tq) - (KC // KB) * c - r, 0) for r in range(KC // KB)]
        groups = range(H_C // 2)
        cols = [slice(g * pair, (g + 1) * pair) for g in groups]
        s = [_dot_nt(kc, qm_ref[cols[g], :])
             + jnp.concatenate([bias_ref[t, :, cols[g]] for t in tiles], axis=0) + off for g in groups]
        m_prev = [m_ref[:, cols[g]] for g in groups]
        m_new = [jnp.maximum(m_prev[g], jnp.max(s[g], axis=0, keepdims=True)) for g in groups]
        p = [jnp.exp(s[g] - m_new[g]) for g in groups]
        alpha = [jnp.exp(m_prev[g] - m_new[g]) for g in groups]
        pv = [_dot(vt, p[g].astype(BF16)) for g in groups]
        for g in groups:
            l_ref[:, cols[g]] = alpha[g] * l_ref[:, cols[g]] + jnp.sum(p[g], axis=0, keepdims=True)
            acc_ref[:, cols[g]] = alpha[g] * acc_ref[:, cols[g]] + pv[g]
            m_ref[:, cols[g]] = m_new[g]
        return carry

    lax.fori_loop(0, nk, attend, 0)
    o = acc_ref[...] / l_ref[...]
    _store_heads_token_major(o_ref, [o[:, h * tq:(h + 1) * tq] for h in range(H_C)])


def _dsa_attention(q, qi, k2, ki2, v2, wi_t, bias, tq, q_off, kv_len):
    tk = k2.shape[1]
    nq = q.shape[1] // tq
    hq = H_C * tq
    return _attn_call(
        functools.partial(_dsa_kernel, q_off=q_off, kv_len=kv_len), "dsa_attention", tq,
        (q, qi), (k2, ki2, v2),
        [pltpu.VMEM((hq, LANE), BF16), pltpu.VMEM((hq, LANE), BF16), pltpu.VMEM((LANE, tk), BF16),
         pltpu.VMEM((tk, tq), jnp.int32), pltpu.VMEM((tk, tq), jnp.int16), pltpu.VMEM((tk, tq), jnp.int16),
         pltpu.VMEM((1, hq), F32), pltpu.VMEM((1, hq), F32),
         pltpu.VMEM((D_C, hq), F32)],
        extra_in=(wi_t, bias),
        extra_specs=(pl.BlockSpec((WI_ROWS, tq), lambda i, j: (0, i * nq + j)), _const_spec(bias.shape)))


def _merge_kernel(x_ref, oa_ref, ob_ref, oc_ref, g_ref, wg_ref, wbr_ref, wo_ref, post_ref, o_ref):
    x = x_ref[...]
    u = _rms(x, g_ref[...]).astype(BF16)
    merged = jnp.zeros(x.shape, F32)
    for i, o_br in enumerate((oa_ref, ob_ref, oc_ref)):
        gate = jax.nn.sigmoid(_dot(u, wg_ref[:, i * D_MODEL:(i + 1) * D_MODEL]))
        merged = merged + gate * _dot(o_br[...], wbr_ref[i])
    y = _dot(merged.astype(BF16), wo_ref[...])
    o_ref[...] = x + _rms(y, post_ref[...])


def _merge(x, oa, ob, oc, g, w_gate, w_br, w_o, post_g, tm):
    n = x.shape[0]
    row = pl.BlockSpec((tm, D_MODEL), lambda i: (i, 0))
    br = pl.BlockSpec((tm, MIX_W), lambda i: (i, 0))
    return pl.pallas_call(
        _merge_kernel,
        grid=(n // tm,),
        in_specs=[row, br, br, br, _const_spec((1, D_MODEL)), _const_spec(w_gate.shape),
                  _const_spec(w_br.shape), _const_spec(w_o.shape), _const_spec((1, D_MODEL))],
        out_specs=row,
        out_shape=jax.ShapeDtypeStruct((n, D_MODEL), F32),
        compiler_params=_cparams("parallel"),
        name="mixer_merge",
    )(x, oa, ob, oc, g, w_gate, w_br, w_o, post_g)


GATHER_WINDOW = 3 * LANE


def _gather_cols_kernel(start_ref, w_ref, rel_ref, o_ref):
    start = pl.multiple_of(start_ref[pl.program_id(0)], LANE)
    window = w_ref[:, pl.ds(start, GATHER_WINDOW)]
    pick = jnp.where(lax.broadcasted_iota(jnp.int32, (GATHER_WINDOW, LANE), 0) == rel_ref[0], 1.0, 0.0)
    o_ref[...] = _dot(window, pick.astype(BF16)).astype(BF16)


def _gather_cols(w, src):
    src = np.asarray(src, np.int64).reshape(-1, LANE)
    lo = np.where(src >= 0, src, np.iinfo(np.int64).max).min(axis=1)
    lo = np.where(lo == np.iinfo(np.int64).max, 0, lo)
    start = (lo // LANE) * LANE
    rel = np.where(src >= 0, src - start[:, None], -1)
    assert rel.max() < GATHER_WINDOW
    rows, cols = w.shape
    padded = -(-(int(start.max()) + GATHER_WINDOW) // LANE) * LANE
    w = jnp.pad(w.astype(BF16), ((0, 0), (0, max(0, padded - cols))))
    return pl.pallas_call(
        _gather_cols_kernel,
        grid_spec=pltpu.PrefetchScalarGridSpec(
            num_scalar_prefetch=1,
            grid=(src.shape[0],),
            in_specs=[pl.BlockSpec(w.shape, lambda t, start: (0, 0), pipeline_mode=pl.Buffered(1)),
                      pl.BlockSpec((1, 1, LANE), lambda t, start: (t, 0, 0))],
            out_specs=pl.BlockSpec((rows, LANE), lambda t, start: (0, t))),
        out_shape=jax.ShapeDtypeStruct((rows, src.size), BF16),
        compiler_params=_cparams("parallel"),
        name="gather_weight_columns",
    )(jnp.asarray(start, jnp.int32), w, jnp.asarray(rel.reshape(-1, 1, LANE), jnp.int32))


def _in_proj_sources():
    off = dict(zip(('q_lat', 'kv_lat', 'k_r', 'q_b', 'k_b', 'v_b', 'q_c', 'k_c', 'v_c', 'q_i', 'k_i', 'w_i',
                    'g_a', 'g_b', 'g_c'), IN_OFFSETS))

    def piece(name, width):
        return np.arange(off[name], off[name] + width)

    k_r = piece('k_r', A_ROPE)
    k_r_swapped = np.concatenate([k_r[A_ROPE // 2:], k_r[:A_ROPE // 2]])
    pack = np.concatenate(
        [piece('q_lat', Q_LORA), piece('kv_lat', KV_LORA), np.tile(k_r, LANE // A_ROPE),
         np.tile(k_r_swapped, LANE // A_ROPE), piece('q_b', MIX_W), piece('k_b', MIX_W), piece('v_b', MIX_W),
         piece('q_c', MIX_W), piece('q_i', MIX_W), np.tile(piece('k_c', D_C), 2), np.tile(piece('v_c', D_C), 2),
         np.tile(piece('k_i', D_IDX), 2)])
    assert pack.size == _P_END
    return pack, piece('g_a', 3 * D_MODEL)


def _uq_sources():
    head = np.arange(H_A)[:, None] * (A_NOPE + A_ROPE)
    nope = head + np.arange(A_NOPE)[None, :]
    rope = head + A_NOPE + np.arange(A_ROPE)[None, :]
    swapped = np.concatenate([rope[:, A_ROPE // 2:], rope[:, :A_ROPE // 2]], axis=1)
    return np.concatenate([nope.ravel(), rope.ravel(), swapped.ravel()])


def _ukv_sources():
    head = np.arange(H_A)[:, None] * (A_NOPE + A_V)
    return np.concatenate([(head + np.arange(A_NOPE)[None, :]).ravel(),
                           (head + A_NOPE + np.arange(A_V)[None, :]).ravel()])


def _pack_layer(P, l):
    w_in = P['w_in'][l]
    src_pack, src_gate = _in_proj_sources()
    w_pack = _gather_cols(w_in, src_pack)
    w_gate = _gather_cols(w_in, src_gate)
    w_i = w_in[:, IN_OFFSETS[11]:IN_OFFSETS[12]]
    wwi_t = jnp.pad(w_i.T, ((0, WI_ROWS - H_IDX), (0, 0))).astype(BF16)
    wuq_pack = _gather_cols(P['w_mla_uq'][l], _uq_sources())
    wukv_pack = _gather_cols(P['w_mla_ukv'][l], _ukv_sources())
    return dict(
        w_pack=w_pack, wwi_t=wwi_t, wuq_pack=wuq_pack, wukv_pack=wukv_pack,
        w_gate=w_gate,
        w_br=P['w_branch'][l].astype(BF16), w_o=P['w_o'][l].astype(BF16),
        mix_pre=P['mix_pre_gain'][l][None], mix_post=P['mix_post_gain'][l][None],
        q_norm=P['mla_q_norm'][l][None], kv_norm=P['mla_kv_norm'][l][None],
        ffn=[dict(pre=P['ffn_pre_gain'][l, i][None], post=P['ffn_post_gain'][l, i][None],
                  w_up=P['w_ffn_up'][l, i].astype(BF16), w_down=P['w_ffn_down'][l, i].astype(BF16))
             for i in range(2)])


def _rope_tables(pos, rows):
    half = A_ROPE // 2
    inv = ROPE_THETA ** (-jnp.arange(half, dtype=F32) / half)
    ang = pos.astype(F32)[:, None] * inv[None, :]
    c, s = jnp.cos(ang), jnp.sin(ang)
    ck = jnp.concatenate([c, c], axis=1)
    sk = jnp.concatenate([-s, s], axis=1)
    rep = max(1, rows // pos.shape[0])
    ck, sk = jnp.tile(ck, (rep, 1)), jnp.tile(sk, (rep, 1))
    return (jnp.tile(ck, (1, H_A)), jnp.tile(sk, (1, H_A)),
            jnp.tile(ck, (1, LANE // A_ROPE)), jnp.tile(sk, (1, LANE // A_ROPE)))


def _pad_axis1(a, size):
    if a.shape[1] == size:
        return a
    pad = [(0, 0)] * a.ndim
    pad[1] = (0, size - a.shape[1])
    return jnp.pad(a, pad)


def _mixer(x, L, past, bias_tiles, b, t, tm, tq):
    n = b * t
    p_len = 0 if past is None else past[0].shape[1]
    kv_len = p_len + t
    tk = -(-kv_len // KC) * KC
    tqp = -(-t // tq) * tq
    pos = p_len + jnp.arange(t, dtype=jnp.int32)
    cq, sq, ck, sk = _rope_tables(pos, tm)
    (ckv, krope, kb, vb, kc, vc, ki, wi_t, qan, qar, qb, qc, qi, kr4, kb16, vb16, kc2, vc2, ki2) = _mixer_in(
        x, L['mix_pre'], L['w_pack'], L['wwi_t'], L['q_norm'], L['wuq_pack'], L['kv_norm'], cq, sq, ck, sk, tm)
    new_rows = (ckv.reshape(b, t, KV_LORA), krope.reshape(b, t, A_ROPE),
                kb.reshape(b, t, H_B, D_B), vb.reshape(b, t, H_B, D_B),
                kc.reshape(b, t, D_C), vc.reshape(b, t, D_C), ki.reshape(b, t, D_IDX))

    def keys(a):
        return _pad_axis1(a, tk)

    if past is None:
        ckv_all = keys(new_rows[0])
        kr4_all, kb_all, vb_all, kc_all, vc_all, ki_all = (
            keys(a.reshape(b, t, -1)) for a in (kr4, kb16, vb16, kc2, vc2, ki2))
    else:
        def with_past(pa, nr, w):
            return keys(jnp.concatenate([pa.reshape(b, p_len, w), nr.reshape(b, t, w)], axis=1))

        def lanes(a, rep):
            return jnp.tile(a.astype(BF16), (1, 1, rep))

        ckv_all = with_past(past[0], ckv, KV_LORA)
        kr4_all = lanes(with_past(past[1], krope, A_ROPE), LANE // A_ROPE)
        kb_all = with_past(past[2], kb, MIX_W).astype(BF16)
        vb_all = with_past(past[3], vb, MIX_W).astype(BF16)
        kc_all, vc_all, ki_all = (lanes(with_past(pa, nr, w), LANE // w)
                                  for pa, nr, w in ((past[4], kc, D_C), (past[5], vc, D_C), (past[6], ki, D_IDX)))

    def queries(a):
        return _pad_axis1(a.reshape(b, t, -1), tqp)

    def attended(o):
        return o[:, :t].reshape(n, MIX_W)

    k_nope, v_a = _ukv(ckv_all.reshape(b * tk, KV_LORA), L['wukv_pack'], 1024 if (b * tk) % 1024 == 0 else tk)
    o_a = _mla_attention(queries(qan), queries(qar), k_nope.reshape(b, tk, MIX_W), kr4_all,
                         v_a.reshape(b, tk, MIX_W), tq, p_len, kv_len)
    o_b = _sb_attention(queries(qb), kb_all, vb_all, tq, p_len, kv_len)
    wi_t = _pad_axis1(wi_t.reshape(WI_ROWS * b, t), tqp).reshape(WI_ROWS, b * tqp)
    o_c = _dsa_attention(queries(qc), queries(qi), kc_all, ki_all, vc_all, wi_t, bias_tiles, tq, p_len, kv_len)
    x = _merge(x, attended(o_a), attended(o_b), attended(o_c), L['mix_pre'], L['w_gate'], L['w_br'], L['w_o'],
               L['mix_post'], tm)
    return x, new_rows


def _trunk(x, past, layers, rel_bias):
    b, t, _ = x.shape
    n = b * t
    tm = min(512, n)
    p_len = 0 if past is None else past[0][0].shape[1]
    tq = min(QUERY_TILE_CAP, -(-t // KB) * KB)
    assert p_len % KB == 0 and tq % KB == 0
    n_off = (p_len + (-(-t // tq) - 1) * tq) // KB + 1 + _bias_d0(tq)
    bias_tiles = _bias_tiles(rel_bias, tq, n_off)
    x = x.reshape(n, D_MODEL)
    rows = []
    for l, L in enumerate(layers):
        f = L['ffn'][0]
        x = _ffn_half(x, f['pre'], f['w_up'], f['w_down'], f['post'], tm)
        x, new = _mixer(x, L, None if past is None else past[l], bias_tiles, b, t, tm, tq)
        f = L['ffn'][1]
        x = _ffn_half(x, f['pre'], f['w_up'], f['w_down'], f['post'], tm)
        rows.append(new)
    return x.reshape(b, t, D_MODEL), tuple(jnp.stack(g) for g in zip(*rows))


def kernel(x_prompt, x_sample, cache_mla_ckv, cache_mla_krope, cache_sb_k, cache_sb_v, cache_dsa_k, cache_dsa_v, cache_dsa_kidx, rel_bias, ffn_pre_gain, w_ffn_up, w_ffn_down, ffn_post_gain, mix_pre_gain, w_in, mla_q_norm, w_mla_uq, mla_kv_norm, w_mla_ukv, w_branch, w_o, mix_post_gain):
    P = dict(ffn_pre_gain=ffn_pre_gain, w_ffn_up=w_ffn_up, w_ffn_down=w_ffn_down,
             ffn_post_gain=ffn_post_gain, mix_pre_gain=mix_pre_gain, w_in=w_in,
             mla_q_norm=mla_q_norm, w_mla_uq=w_mla_uq, mla_kv_norm=mla_kv_norm,
             w_mla_ukv=w_mla_ukv, w_branch=w_branch, w_o=w_o, mix_post_gain=mix_post_gain)
    depth = w_in.shape[0]
    layers = [_pack_layer(P, l) for l in range(depth)]
    y_prompt, p_rows = _trunk(x_prompt, None, layers, rel_bias)
    caches = (cache_mla_ckv, cache_mla_krope, cache_sb_k, cache_sb_v,
              cache_dsa_k, cache_dsa_v, cache_dsa_kidx)
    past = [tuple(c[l] for c in caches) for l in range(depth)]
    y_sample, s_rows = _trunk(x_sample, past, layers, rel_bias)
    return (y_prompt, y_sample) + p_rows + s_rows
```
